```python
import jax, jax.numpy as jnp
from jax import lax
import numpy as np

D_MODEL = 1024
BATCH = 16
SEQ = 2048
DEPTH = 1

POOL_WIDTH = D_MODEL // 2
POOL_WINDOWS = (2, 4, 8, 16)
N_POOL_GROUPS = len(POOL_WINDOWS)
POOL_GROUP = POOL_WIDTH // N_POOL_GROUPS
SSD_HEAD_DIM = 64
SSD_INNER = D_MODEL
SSD_HEADS = SSD_INNER // SSD_HEAD_DIM
SSD_GROUPS = 2
SSD_HPG = SSD_HEADS // SSD_GROUPS
SSD_STATE = 128
CONV_WIDTH = 4
CHUNK = 128
CONV_CH = SSD_INNER + 2 * SSD_GROUPS * SSD_STATE
MIX_WIDTH = POOL_WIDTH + SSD_INNER
OFF_POOL = 0
OFF_Z = OFF_POOL + POOL_WIDTH
OFF_XBC = OFF_Z + SSD_INNER
OFF_DT = OFF_XBC + CONV_CH
IN_WIDTH = OFF_DT + SSD_HEADS
D_FF = 4 * D_MODEL
N_MOD = 6
EPS = 1e-5

kernel_name = "hybrid_pool_ssd_adaln_block"


def rms_norm(x, g):
    x32 = x.astype(jnp.float32)
    y = x32 * lax.rsqrt(jnp.mean(x32 * x32, axis=-1, keepdims=True) + EPS)
    return (y * g.astype(jnp.float32)).astype(x.dtype)


def pool_mixer(u, w_pool, pool_scale):
    Bsz, S, _ = u.shape
    u32 = u.astype(jnp.float32)
    cs = jnp.cumsum(u32, axis=1)
    t = jnp.arange(1, S + 1, dtype=jnp.float32)[None, :, None]
    outs = []
    for gi, w in enumerate(POOL_WINDOWS):
        sl = slice(gi * POOL_GROUP, (gi + 1) * POOL_GROUP)
        cs_g = cs[..., sl]
        prev = jnp.pad(cs_g, ((0, 0), (w, 0), (0, 0)))[:, :S]
        mean = (cs_g - prev) / jnp.minimum(t, float(w))
        outs.append(mean - u32[..., sl])
    p = jnp.stack(outs, axis=2).astype(u.dtype)
    y = jnp.einsum('bsgc,gcd->bsgd', p, w_pool).reshape(Bsz, S, POOL_WIDTH)
    return y * pool_scale


def causal_depthwise_conv(u, w, b):
    K = w.shape[0]
    S = u.shape[1]
    up = jnp.pad(u, ((0, 0), (K - 1, 0), (0, 0)))
    out = b + up[:, 0:S] * w[0]
    for k in range(1, K):
        out = out + up[:, k:k + S] * w[k]
    return out


def ssd_chunked(xs, dt, a, bm, cm):
    Bsz, S, G, R, P = xs.shape
    N = bm.shape[-1]
    nc = S // CHUNK
    xdt = (xs.astype(jnp.float32) * dt[..., None]).reshape(Bsz, nc, CHUNK, G, R, P)
    da = (dt * a).reshape(Bsz, nc, CHUNK, G, R)
    bc = bm.astype(jnp.float32).reshape(Bsz, nc, CHUNK, G, N)
    cc = cm.astype(jnp.float32).reshape(Bsz, nc, CHUNK, G, N)
    a_cum = jnp.cumsum(da, axis=2)
    causal = jnp.tril(jnp.ones((CHUNK, CHUNK), dtype=bool))[None, None, :, :, None, None]
    seg = a_cum[:, :, :, None] - a_cum[:, :, None, :]
    decay_in = jnp.exp(jnp.where(causal, seg, -jnp.inf))
    scores = jnp.einsum('bclgn,bcsgn->bclsg', cc, bc)
    y_diag = jnp.einsum('bclsgr,bcsgrp->bclgrp', scores[..., None] * decay_in, xdt)
    decay_out = jnp.exp(a_cum[:, :, -1:] - a_cum)
    states = jnp.einsum('bclgn,bclgr,bclgrp->bcgrpn', bc, decay_out, xdt)
    chunk_decay = jnp.exp(a_cum[:, :, -1])

    def step(h, inp):
        st, dec = inp
        return h * dec[..., None, None] + st, h

    h0 = jnp.zeros((Bsz, G, R, P, N), jnp.float32)
    _, prev = lax.scan(step, h0, (jnp.moveaxis(states, 1, 0), jnp.moveaxis(chunk_decay, 1, 0)))
    prev = jnp.moveaxis(prev, 0, 1)
    y_off = jnp.einsum('bclgn,bcgrpn,bclgr->bclgrp', cc, prev, jnp.exp(a_cum))
    return (y_diag + y_off).reshape(Bsz, S, G, R, P)


def ssd_mixer(z, u_xbc, u_dt, conv_w, conv_b, dt_bias, a_log, d_skip, g_ssd):
    Bsz, S, _ = z.shape
    xbc = jax.nn.silu(causal_depthwise_conv(u_xbc, conv_w, conv_b))
    GN = SSD_GROUPS * SSD_STATE
    xs = xbc[..., :SSD_INNER].reshape(Bsz, S, SSD_GROUPS, SSD_HPG, SSD_HEAD_DIM)
    bm = xbc[..., SSD_INNER:SSD_INNER + GN].reshape(Bsz, S, SSD_GROUPS, SSD_STATE)
    cm = xbc[..., SSD_INNER + GN:].reshape(Bsz, S, SSD_GROUPS, SSD_STATE)
    dt = jax.nn.softplus(u_dt.astype(jnp.float32) + dt_bias.astype(jnp.float32))
    dt = dt.reshape(Bsz, S, SSD_GROUPS, SSD_HPG)
    a = -jnp.exp(a_log.astype(jnp.float32)).reshape(SSD_GROUPS, SSD_HPG)
    y = ssd_chunked(xs, dt, a, bm, cm)
    y = y + d_skip.astype(jnp.float32).reshape(SSD_GROUPS, SSD_HPG)[:, :, None] * xs.astype(jnp.float32)
    y = y.reshape(Bsz, S, SSD_INNER) * jax.nn.silu(z.astype(jnp.float32))
    yg = y.reshape(Bsz, S, SSD_GROUPS, SSD_INNER // SSD_GROUPS)
    yg = yg * lax.rsqrt(jnp.mean(yg * yg, axis=-1, keepdims=True) + EPS)
    y = yg.reshape(Bsz, S, SSD_INNER) * g_ssd.astype(jnp.float32)
    return y.astype(z.dtype)


def setup_inputs(seed: int = 0) -> dict:
    key = jax.random.key(seed)
    ks = jax.random.split(key, 20)
    f32 = jnp.float32
    L = DEPTH
    x = jax.random.normal(ks[0], (BATCH, SEQ, D_MODEL), f32)
    c = jax.random.normal(ks[1], (BATCH, D_MODEL), f32)
    w_ada = jax.random.normal(ks[2], (L, D_MODEL, N_MOD * D_MODEL), f32) * D_MODEL ** -0.5
    b_ada = jax.random.normal(ks[3], (L, N_MOD * D_MODEL), f32) * 0.02
    g_mix = 1.0 + 0.02 * jax.random.normal(ks[4], (L, D_MODEL), f32)
    w_in = jax.random.normal(ks[5], (L, D_MODEL, IN_WIDTH), f32) * D_MODEL ** -0.5
    conv_w = jax.random.normal(ks[6], (L, CONV_WIDTH, CONV_CH), f32) * CONV_WIDTH ** -0.5
    conv_b = jax.random.normal(ks[7], (L, CONV_CH), f32) * 0.02
    dt0 = jnp.exp(jax.random.uniform(ks[8], (L, SSD_HEADS), f32, np.log(1e-3), np.log(1e-1)))
    dt_bias = dt0 + jnp.log(-jnp.expm1(-dt0))
    a_log = jnp.log(jax.random.uniform(ks[9], (L, SSD_HEADS), f32, 1.0, 16.0))
    d_skip = 1.0 + 0.1 * jax.random.normal(ks[10], (L, SSD_HEADS), f32)
    g_ssd = 1.0 + 0.02 * jax.random.normal(ks[11], (L, SSD_INNER), f32)
    w_pool = jax.random.normal(ks[12], (L, N_POOL_GROUPS, POOL_GROUP, POOL_GROUP), f32) * POOL_GROUP ** -0.5
    pool_scale = 1.0 + 0.1 * jax.random.normal(ks[13], (L, POOL_WIDTH), f32)
    w_out = jax.random.normal(ks[14], (L, MIX_WIDTH, D_MODEL), f32) * MIX_WIDTH ** -0.5
    g_mlp = 1.0 + 0.02 * jax.random.normal(ks[15], (L, D_MODEL), f32)
    w_up = jax.random.normal(ks[16], (L, D_MODEL, D_FF), f32) * D_MODEL ** -0.5
    w_down = jax.random.normal(ks[17], (L, D_FF, D_MODEL), f32) * D_FF ** -0.5
    g_final = 1.0 + 0.02 * jax.random.normal(ks[18], (D_MODEL,), f32)
    return {"x": x, "c": c, "w_ada": w_ada, "b_ada": b_ada, "g_mix": g_mix, "w_in": w_in,
            "conv_w": conv_w, "conv_b": conv_b, "dt_bias": dt_bias, "a_log": a_log,
            "d_skip": d_skip, "g_ssd": g_ssd, "w_pool": w_pool, "pool_scale": pool_scale,
            "w_out": w_out, "g_mlp": g_mlp, "w_up": w_up, "w_down": w_down, "g_final": g_final}


def reference(x, c, w_ada, b_ada, g_mix, w_in, conv_w, conv_b, dt_bias, a_log, d_skip, g_ssd,
              w_pool, pool_scale, w_out, g_mlp, w_up, w_down, g_final):
    h = x
    c_act = jax.nn.silu(c)
    for layer in range(DEPTH):
        mod = jnp.einsum('bd,de->be', c_act, w_ada[layer]) + b_ada[layer]
        shift_m, scale_m, gate_m, shift_f, scale_f, gate_f = jnp.split(mod[:, None, :], N_MOD, axis=-1)

        u = rms_norm(h, g_mix[layer]) * (1.0 + scale_m) + shift_m
        proj = jnp.einsum('bsd,de->bse', u, w_in[layer])
        u_pool = proj[..., OFF_POOL:OFF_Z]
        z = proj[..., OFF_Z:OFF_XBC]
        u_xbc = proj[..., OFF_XBC:OFF_DT]
        u_dt = proj[..., OFF_DT:]
        y_pool = pool_mixer(u_pool, w_pool[layer], pool_scale[layer])
        y_ssd = ssd_mixer(z, u_xbc, u_dt, conv_w[layer], conv_b[layer], dt_bias[layer],
                          a_log[layer], d_skip[layer], g_ssd[layer])
        y_mix = jnp.concatenate([y_pool.astype(h.dtype), y_ssd.astype(h.dtype)], axis=-1)
        h = h + gate_m * jnp.einsum('bse,ed->bsd', y_mix, w_out[layer])

        u = rms_norm(h, g_mlp[layer]) * (1.0 + scale_f) + shift_f
        f = jnp.square(jax.nn.relu(jnp.einsum('bsd,df->bsf', u, w_up[layer])))
        h = h + gate_f * jnp.einsum('bsf,fd->bsd', f, w_down[layer])
    return rms_norm(h, g_final)
```

```python
import functools

import jax
import jax.numpy as jnp
from jax import lax
from jax.experimental import pallas as pl
from jax.experimental.pallas import tpu as pltpu

F32 = jnp.float32
BF16 = jnp.bfloat16

D_MODEL = 1024
POOL_WIDTH = 512
POOL_WINDOWS = (2, 4, 8, 16)
POOL_GROUP = 128
SSD_INNER = 1024
SSD_HEADS = 16
SSD_HEAD_DIM = 64
SSD_GROUPS = 2
SSD_STATE = 128
GROUP_WIDTH = SSD_INNER // SSD_GROUPS
CONV_WIDTH = 4
CHUNK = 128
CONV_CH = SSD_INNER + 2 * SSD_GROUPS * SSD_STATE
MIX_WIDTH = POOL_WIDTH + SSD_INNER
OFF_Z = POOL_WIDTH
OFF_XBC = OFF_Z + SSD_INNER
OFF_DT = OFF_XBC + CONV_CH
LANES = 128
IN_PAD = OFF_DT + LANES
D_FF = 4096
N_MOD = 6
EPS = 1e-5

POOL_HIST = 16
CONV_HIST = 8
MIX_TILE = 512
MLP_TILE = 512
FF_CHUNK = 1024
ROW_BLOCK = 64
VMEM_LIMIT = 56 * 1024 * 1024


def _const_spec(shape):
    zeros = (0,) * len(shape)
    return pl.BlockSpec(shape, lambda *_: zeros, pipeline_mode=pl.Buffered(1))


def _mod_kernel(c_ref, w_ref, b_ref, o_ref):
    c = c_ref[...]
    c_act = (c * jax.nn.sigmoid(c)).astype(BF16)
    o_ref[...] = jnp.dot(c_act, w_ref[...].astype(BF16), preferred_element_type=F32) + b_ref[...]


def _modulation(c, w_ada, b_ada):
    batch = c.shape[0]
    n_out = w_ada.shape[1]
    return pl.pallas_call(
        _mod_kernel,
        grid=(n_out // D_MODEL,),
        in_specs=[
            pl.BlockSpec((batch, D_MODEL), lambda j: (0, 0)),
            pl.BlockSpec((D_MODEL, D_MODEL), lambda j: (0, j)),
            pl.BlockSpec((1, D_MODEL), lambda j: (0, j)),
        ],
        out_specs=pl.BlockSpec((batch, D_MODEL), lambda j: (0, j)),
        out_shape=jax.ShapeDtypeStruct((batch, n_out), F32),
        name="adaln_mod",
    )(c, w_ada, b_ada.reshape(1, n_out))


def _split_hi_lo(v):
    hi = v.astype(BF16)
    lo = (v - hi.astype(F32)).astype(BF16)
    return jnp.concatenate([hi, lo], axis=1)


def _mixer_kernel(x_ref, mod_ref, gmix_ref, win_ref, convw_ref, convb_ref, dtb_ref, alog_ref,
                  dskip_ref, gssd_ref, wpool_ref, pscale_ref, wout_ref, expand_ref, o_ref,
                  u_buf, pool_buf, z_buf, xbc_buf, xs_buf, bc_buf, dtx_buf, acx_buf, ac_buf,
                  act_buf, ymix_buf, state_buf):
    ts = x_ref.shape[0]
    n_chunks = ts // CHUNK
    seq_tile = pl.program_id(1)

    @pl.when(seq_tile == 0)
    def _():
        pool_buf[0:POOL_HIST, :] = jnp.zeros((POOL_HIST, POOL_WIDTH), F32)
        xbc_buf[0:CONV_HIST, :] = jnp.zeros((CONV_HIST, CONV_CH), F32)
        state_buf[...] = jnp.zeros(state_buf.shape, F32)

    gain = gmix_ref[...] * (1.0 + mod_ref[1:2, :])
    shift = mod_ref[0:1, :]

    def norm_body(i, carry):
        r = pl.multiple_of(i * ROW_BLOCK, ROW_BLOCK)
        xb = x_ref[pl.ds(r, ROW_BLOCK), :]
        ms = jnp.mean(xb * xb, axis=-1, keepdims=True)
        u_buf[pl.ds(r, ROW_BLOCK), :] = (xb * lax.rsqrt(ms + EPS) * gain + shift).astype(BF16)
        return carry

    lax.fori_loop(0, ts // ROW_BLOCK, norm_body, 0)

    u = u_buf[...]
    xbc_buf[CONV_HIST:CONV_HIST + ts, :] = jnp.dot(
        u, win_ref[:, OFF_XBC:OFF_DT], preferred_element_type=F32)
    dt_raw = jnp.dot(u, win_ref[:, OFF_DT:IN_PAD], preferred_element_type=F32)
    pool_buf[POOL_HIST:POOL_HIST + ts, :] = jnp.dot(
        u, win_ref[:, 0:OFF_Z], preferred_element_type=F32)
    z_buf[...] = jnp.dot(u, win_ref[:, OFF_Z:OFF_XBC], preferred_element_type=F32)

    dt = jax.nn.softplus(dt_raw + dtb_ref[...])
    da = dt * (-jnp.exp(alog_ref[...]))
    row_id = lax.broadcasted_iota(jnp.int32, (CHUNK, CHUNK), 0)
    col_id = lax.broadcasted_iota(jnp.int32, (CHUNK, CHUNK), 1)
    causal = row_id >= col_id
    tril = causal.astype(F32)
    for c in range(n_chunks):
        rows = slice(c * CHUNK, (c + 1) * CHUNK)
        ac = jnp.dot(tril, da[rows], precision=lax.Precision.HIGHEST,
                     preferred_element_type=F32)
        ac_buf[rows, :] = ac
        act_buf[c * SSD_HEADS:(c + 1) * SSD_HEADS, :] = ac.T[0:SSD_HEADS, :]
    dtx_buf[...] = jnp.dot(_split_hi_lo(dt), expand_ref[...], preferred_element_type=F32)
    acx_buf[...] = jnp.dot(_split_hi_lo(ac_buf[...]), expand_ref[...], preferred_element_type=F32)

    tpos = (seq_tile * ts + 1 + lax.broadcasted_iota(jnp.int32, (ts, LANES), 0)).astype(F32)
    for g, w in enumerate(POOL_WINDOWS):
        lanes = slice(g * POOL_GROUP, (g + 1) * POOL_GROUP)
        cur = pool_buf[POOL_HIST:POOL_HIST + ts, lanes]
        acc = cur
        for k in range(1, w):
            acc = acc + pool_buf[POOL_HIST - k:POOL_HIST - k + ts, lanes]
        p = acc / jnp.minimum(tpos, float(w)) - cur
        yp = jnp.dot(p.astype(BF16), wpool_ref[g], preferred_element_type=F32)
        ymix_buf[:, lanes] = (yp * pscale_ref[:, lanes]).astype(BF16)
    pool_buf[0:POOL_HIST, :] = pool_buf[ts:ts + POOL_HIST, :]

    first = CONV_HIST - (CONV_WIDTH - 1)
    for i in range(ts // ROW_BLOCK):
        r = i * ROW_BLOCK
        acc = convb_ref[...] + xbc_buf[first + r:first + r + ROW_BLOCK, :] * convw_ref[0:1, :]
        for k in range(1, CONV_WIDTH):
            acc = acc + xbc_buf[first + k + r:first + k + r + ROW_BLOCK, :] * convw_ref[k:k + 1, :]
        act = acc * jax.nn.sigmoid(acc)
        xs_buf[r:r + ROW_BLOCK, :] = act[:, 0:SSD_INNER]
        bc_buf[r:r + ROW_BLOCK, :] = act[:, SSD_INNER:].astype(BF16)
    xbc_buf[0:CONV_HIST, :] = xbc_buf[ts:ts + CONV_HIST, :]

    lane_id = lax.broadcasted_iota(jnp.int32, (CHUNK, LANES), 1)
    low_half = lane_id < SSD_HEAD_DIM
    bc_off = SSD_GROUPS * SSD_STATE

    def chunk_body(c, carry):
        r0 = pl.multiple_of(c * CHUNK, CHUNK)
        rows = pl.ds(r0, CHUNK)
        xs = xs_buf[rows, :]
        acx = acx_buf[rows, :]
        a_last = acx_buf[pl.ds(r0 + CHUNK - 1, 1), :]
        xdt = xs * dtx_buf[rows, :]
        xdt_bf = xdt.astype(BF16)
        xdec_bf = (xdt * jnp.exp(a_last - acx)).astype(BF16)
        e_cum = jnp.exp(acx)
        chunk_decay = jnp.exp(a_last)
        ac = ac_buf[rows, :]
        act = act_buf[pl.ds(pl.multiple_of(c * SSD_HEADS, SSD_HEADS), SSD_HEADS), :]
        y_groups = []
        for g in range(SSD_GROUPS):
            gl = slice(g * GROUP_WIDTH, (g + 1) * GROUP_WIDTH)
            b_g = bc_buf[rows, g * SSD_STATE:(g + 1) * SSD_STATE]
            c_g = bc_buf[rows, bc_off + g * SSD_STATE:bc_off + (g + 1) * SSD_STATE]
            scores = lax.dot_general(c_g, b_g, (((1,), (1,)), ((), ())),
                                     preferred_element_type=F32)
            pair_out = []
            for j in range(GROUP_WIDTH // LANES):
                m_pair = []
                for hh in range(2):
                    h = g * (SSD_HEADS // SSD_GROUPS) + 2 * j + hh
                    seg = ac[:, h:h + 1] - act[h:h + 1, :]
                    decay = jnp.exp(jnp.where(causal, seg, -jnp.inf))
                    m_pair.append((scores * decay).astype(BF16))
                lhs = jnp.concatenate(m_pair, axis=1)
                xp = xdt_bf[:, g * GROUP_WIDTH + j * LANES:g * GROUP_WIDTH + (j + 1) * LANES]
                zero = jnp.zeros_like(xp)
                rhs = jnp.concatenate([jnp.where(low_half, xp, zero),
                                       jnp.where(low_half, zero, xp)], axis=0)
                pair_out.append(jnp.dot(lhs, rhs, preferred_element_type=F32))
            y_diag = jnp.concatenate(pair_out, axis=1)
            prev = state_buf[g]
            y_off = jnp.dot(c_g, prev.astype(BF16), preferred_element_type=F32) * e_cum[:, gl]
            new_states = lax.dot_general(b_g, xdec_bf[:, gl], (((0,), (0,)), ((), ())),
                                         preferred_element_type=F32)
            state_buf[g] = prev * chunk_decay[:, gl] + new_states
            y_groups.append(y_diag + y_off)
        y = jnp.concatenate(y_groups, axis=1) + dskip_ref[...] * xs
        zc = z_buf[rows, :]
        y = y * (zc * jax.nn.sigmoid(zc))
        normed = []
        for g in range(SSD_GROUPS):
            yg = y[:, g * GROUP_WIDTH:(g + 1) * GROUP_WIDTH]
            ms = jnp.mean(yg * yg, axis=-1, keepdims=True)
            normed.append(yg * lax.rsqrt(ms + EPS))
        y = jnp.concatenate(normed, axis=1) * gssd_ref[...]
        ymix_buf[rows, POOL_WIDTH:MIX_WIDTH] = y.astype(BF16)
        return carry

    lax.fori_loop(0, n_chunks, chunk_body, 0)

    mixed = jnp.dot(ymix_buf[...], wout_ref[...], preferred_element_type=F32)
    o_ref[...] = x_ref[...] + mod_ref[2:3, :] * mixed


def _mixer(x, mod3, g_mix, w_in_p, conv_w, conv_b, dtb_p, alog_p, dskip_x, g_ssd, w_pool_b,
           pool_scale, w_out_b, expand):
    batch, seq, _ = x.shape
    ts = min(MIX_TILE, seq)
    tile = lambda b, s: (b, s, 0)
    in_specs = [
        pl.BlockSpec((None, ts, D_MODEL), tile),
        pl.BlockSpec((None, N_MOD, D_MODEL), lambda b, s: (b, 0, 0)),
        _const_spec((1, D_MODEL)),
        _const_spec((D_MODEL, IN_PAD)),
        _const_spec((CONV_WIDTH, CONV_CH)),
        _const_spec((1, CONV_CH)),
        _const_spec((1, LANES)),
        _const_spec((1, LANES)),
        _const_spec((1, SSD_INNER)),
        _const_spec((1, SSD_INNER)),
        _const_spec((len(POOL_WINDOWS), POOL_GROUP, POOL_GROUP)),
        _const_spec((1, POOL_WIDTH)),
        _const_spec((MIX_WIDTH, D_MODEL)),
        _const_spec((2 * LANES, SSD_INNER)),
    ]
    scratch = [
        pltpu.VMEM((ts, D_MODEL), BF16),
        pltpu.VMEM((POOL_HIST + ts, POOL_WIDTH), F32),
        pltpu.VMEM((ts, SSD_INNER), F32),
        pltpu.VMEM((CONV_HIST + ts, CONV_CH), F32),
        pltpu.VMEM((ts, SSD_INNER), F32),
        pltpu.VMEM((ts, 2 * SSD_GROUPS * SSD_STATE), BF16),
        pltpu.VMEM((ts, SSD_INNER), F32),
        pltpu.VMEM((ts, SSD_INNER), F32),
        pltpu.VMEM((ts, LANES), F32),
        pltpu.VMEM((ts // CHUNK * SSD_HEADS, CHUNK), F32),
        pltpu.VMEM((ts, MIX_WIDTH), BF16),
        pltpu.VMEM((SSD_GROUPS, SSD_STATE, GROUP_WIDTH), F32),
    ]
    return pl.pallas_call(
        _mixer_kernel,
        grid=(batch, seq // ts),
        in_specs=in_specs,
        out_specs=pl.BlockSpec((None, ts, D_MODEL), tile),
        out_shape=jax.ShapeDtypeStruct(x.shape, F32),
        scratch_shapes=scratch,
        compiler_params=pltpu.CompilerParams(
            dimension_semantics=("arbitrary", "arbitrary"), vmem_limit_bytes=VMEM_LIMIT),
        name="token_mixer",
    )(x, mod3, g_mix, w_in_p, conv_w, conv_b, dtb_p, alog_p, dskip_x, g_ssd, w_pool_b,
      pool_scale, w_out_b, expand)


def _mlp_kernel(h_ref, mod_ref, gmlp_ref, wup_ref, wdown_ref, gfin_ref, o_ref, u_buf, f_buf, acc_buf):
    tm = h_ref.shape[0]
    gain = gmlp_ref[...] * (1.0 + mod_ref[4:5, :])
    shift = mod_ref[3:4, :]
    gate = mod_ref[5:6, :]

    def norm_body(i, carry):
        r = pl.multiple_of(i * ROW_BLOCK, ROW_BLOCK)
        hb = h_ref[pl.ds(r, ROW_BLOCK), :]
        ms = jnp.mean(hb * hb, axis=-1, keepdims=True)
        u_buf[pl.ds(r, ROW_BLOCK), :] = (hb * lax.rsqrt(ms + EPS) * gain + shift).astype(BF16)
        return carry

    lax.fori_loop(0, tm // ROW_BLOCK, norm_body, 0)

    for j in range(D_FF // FF_CHUNK):
        cols = slice(j * FF_CHUNK, (j + 1) * FF_CHUNK)
        f = jnp.dot(u_buf[...], wup_ref[:, cols], preferred_element_type=F32)
        f_buf[...] = jnp.square(jnp.maximum(f, 0.0)).astype(BF16)
        part = jnp.dot(f_buf[...], wdown_ref[cols, :], preferred_element_type=F32)
        if j == 0:
            acc_buf[...] = part
        else:
            acc_buf[...] += part

    def out_body(i, carry):
        r = pl.multiple_of(i * ROW_BLOCK, ROW_BLOCK)
        hb = h_ref[pl.ds(r, ROW_BLOCK), :] + gate * acc_buf[pl.ds(r, ROW_BLOCK), :]
        ms = jnp.mean(hb * hb, axis=-1, keepdims=True)
        o_ref[pl.ds(r, ROW_BLOCK), :] = hb * lax.rsqrt(ms + EPS) * gfin_ref[...]
        return carry

    lax.fori_loop(0, tm // ROW_BLOCK, out_body, 0)


def _mlp(h, mod3, g_mlp, w_up_b, w_down_b, g_final):
    batch, seq, _ = h.shape
    tm = min(MLP_TILE, seq)
    tile = lambda b, s: (b, s, 0)
    return pl.pallas_call(
        _mlp_kernel,
        grid=(batch, seq // tm),
        in_specs=[
            pl.BlockSpec((None, tm, D_MODEL), tile),
            pl.BlockSpec((None, N_MOD, D_MODEL), lambda b, s: (b, 0, 0)),
            _const_spec((1, D_MODEL)),
            _const_spec((D_MODEL, D_FF)),
            _const_spec((D_FF, D_MODEL)),
            _const_spec((1, D_MODEL)),
        ],
        out_specs=pl.BlockSpec((None, tm, D_MODEL), tile),
        out_shape=jax.ShapeDtypeStruct(h.shape, F32),
        scratch_shapes=[
            pltpu.VMEM((tm, D_MODEL), BF16),
            pltpu.VMEM((tm, FF_CHUNK), BF16),
            pltpu.VMEM((tm, D_MODEL), F32),
        ],
        compiler_params=pltpu.CompilerParams(
            dimension_semantics=("arbitrary", "arbitrary"), vmem_limit_bytes=VMEM_LIMIT),
        name="relu2_mlp",
    )(h, mod3, g_mlp, w_up_b, w_down_b, g_final)


def _expand_matrix():
    head_of_lane = jnp.arange(SSD_INNER, dtype=jnp.int32) // SSD_HEAD_DIM
    row = jnp.arange(2 * LANES, dtype=jnp.int32) % LANES
    return (row[:, None] == head_of_lane[None, :]).astype(BF16)


def _pad_lanes(v):
    return jnp.pad(v.reshape(1, -1), ((0, 0), (0, LANES - v.shape[-1])))


def kernel(x, c, w_ada, b_ada, g_mix, w_in, conv_w, conv_b, dt_bias, a_log, d_skip, g_ssd, w_pool,
           pool_scale, w_out, g_mlp, w_up, w_down, g_final):
    batch = x.shape[0]
    assert w_ada.shape[0] == 1, "the MLP call fuses the final norm, so exactly one layer"
    layer = 0
    mod3 = _modulation(c, w_ada[layer], b_ada[layer]).reshape(batch, N_MOD, D_MODEL)
    w_in_p = jnp.pad(w_in[layer], ((0, 0), (0, IN_PAD - w_in.shape[-1]))).astype(BF16)
    h = _mixer(
        x, mod3, g_mix[layer].reshape(1, -1), w_in_p, conv_w[layer],
        conv_b[layer].reshape(1, -1), _pad_lanes(dt_bias[layer]), _pad_lanes(a_log[layer]),
        jnp.repeat(d_skip[layer], SSD_HEAD_DIM).reshape(1, -1), g_ssd[layer].reshape(1, -1),
        w_pool[layer].astype(BF16), pool_scale[layer].reshape(1, -1),
        w_out[layer].astype(BF16), _expand_matrix())
    return _mlp(h, mod3, g_mlp[layer].reshape(1, -1), w_up[layer].astype(BF16),
                w_down[layer].astype(BF16), g_final.reshape(1, -1))
```

```python
import functools

import jax
import jax.numpy as jnp
from jax import lax
from jax.experimental import pallas as pl
from jax.experimental.pallas import tpu as pltpu

F32 = jnp.float32
BF16 = jnp.bfloat16

D_MODEL = 1024
POOL_WIDTH = 512
POOL_WINDOWS = (2, 4, 8, 16)
POOL_GROUP = 128
SSD_INNER = 1024
SSD_HEADS = 16
SSD_HEAD_DIM = 64
SSD_GROUPS = 2
SSD_STATE = 128
GROUP_WIDTH = SSD_INNER // SSD_GROUPS
CONV_WIDTH = 4
CHUNK = 128
CONV_CH = SSD_INNER + 2 * SSD_GROUPS * SSD_STATE
MIX_WIDTH = POOL_WIDTH + SSD_INNER
OFF_Z = POOL_WIDTH
OFF_XBC = OFF_Z + SSD_INNER
OFF_DT = OFF_XBC + CONV_CH
LANES = 128
IN_PAD = OFF_DT + LANES
D_FF = 4096
N_MOD = 6
EPS = 1e-5

POOL_HIST = 16
CONV_HIST = 8
MIX_TILE = 512
MLP_TILE = 1024
FF_CHUNK = 1024
ROW_BLOCK = 64
VMEM_LIMIT = 56 * 1024 * 1024


def _const_spec(shape):
    zeros = (0,) * len(shape)
    return pl.BlockSpec(shape, lambda *_: zeros, pipeline_mode=pl.Buffered(1))


def _mod_kernel(c_ref, w_ref, b_ref, o_ref):
    c = c_ref[...]
    c_act = (c * jax.nn.sigmoid(c)).astype(BF16)
    o_ref[...] = jnp.dot(c_act, w_ref[...].astype(BF16), preferred_element_type=F32) + b_ref[...]


def _modulation(c, w_ada, b_ada):
    batch = c.shape[0]
    n_out = w_ada.shape[1]
    return pl.pallas_call(
        _mod_kernel,
        grid=(n_out // D_MODEL,),
        in_specs=[
            pl.BlockSpec((batch, D_MODEL), lambda j: (0, 0)),
            pl.BlockSpec((D_MODEL, D_MODEL), lambda j: (0, j)),
            pl.BlockSpec((1, D_MODEL), lambda j: (0, j)),
        ],
        out_specs=pl.BlockSpec((batch, D_MODEL), lambda j: (0, j)),
        out_shape=jax.ShapeDtypeStruct((batch, n_out), F32),
        name="adaln_mod",
    )(c, w_ada, b_ada.reshape(1, n_out))


def _split_hi_lo(v):
    hi = v.astype(BF16)
    lo = (v - hi.astype(F32)).astype(BF16)
    return jnp.concatenate([hi, lo], axis=1)


def _mixer_kernel(x_ref, mod_ref, gmix_ref, win_ref, convw_ref, convb_ref, dtb_ref, alog_ref,
                  dskip_ref, gssd_ref, wpool_ref, pscale_ref, wout_ref, expand_ref, o_ref,
                  u_buf, pool_buf, z_buf, xbc_buf, xs_buf, bc_buf, dtx_buf, acx_buf, ac_buf,
                  act_buf, ymix_buf, state_buf):
    ts = x_ref.shape[0]
    n_chunks = ts // CHUNK
    seq_tile = pl.program_id(1)

    @pl.when(seq_tile == 0)
    def _():
        pool_buf[0:POOL_HIST, :] = jnp.zeros((POOL_HIST, POOL_WIDTH), F32)
        xbc_buf[0:CONV_HIST, :] = jnp.zeros((CONV_HIST, CONV_CH), F32)
        state_buf[...] = jnp.zeros(state_buf.shape, F32)

    gain = gmix_ref[...] * (1.0 + mod_ref[1:2, :])
    shift = mod_ref[0:1, :]

    def norm_body(i, carry):
        r = pl.multiple_of(i * ROW_BLOCK, ROW_BLOCK)
        xb = x_ref[pl.ds(r, ROW_BLOCK), :]
        ms = jnp.mean(xb * xb, axis=-1, keepdims=True)
        u_buf[pl.ds(r, ROW_BLOCK), :] = (xb * lax.rsqrt(ms + EPS) * gain + shift).astype(BF16)
        return carry

    lax.fori_loop(0, ts // ROW_BLOCK, norm_body, 0)

    u = u_buf[...]
    xbc_buf[CONV_HIST:CONV_HIST + ts, :] = jnp.dot(
        u, win_ref[:, OFF_XBC:OFF_DT], preferred_element_type=F32)
    dt_raw = jnp.dot(u, win_ref[:, OFF_DT:IN_PAD], preferred_element_type=F32)
    pool_buf[POOL_HIST:POOL_HIST + ts, :] = jnp.dot(
        u, win_ref[:, 0:OFF_Z], preferred_element_type=F32)
    z_buf[...] = jnp.dot(u, win_ref[:, OFF_Z:OFF_XBC], preferred_element_type=F32)

    dt = jax.nn.softplus(dt_raw + dtb_ref[...])
    da = dt * (-jnp.exp(alog_ref[...]))
    row_id = lax.broadcasted_iota(jnp.int32, (CHUNK, CHUNK), 0)
    col_id = lax.broadcasted_iota(jnp.int32, (CHUNK, CHUNK), 1)
    causal = row_id >= col_id
    tril = causal.astype(F32)
    for c in range(n_chunks):
        rows = slice(c * CHUNK, (c + 1) * CHUNK)
        ac = jnp.dot(tril, da[rows], precision=lax.Precision.HIGHEST,
                     preferred_element_type=F32)
        ac_buf[rows, :] = ac
        act_buf[c * SSD_HEADS:(c + 1) * SSD_HEADS, :] = ac.T[0:SSD_HEADS, :]
    dtx_buf[...] = jnp.dot(_split_hi_lo(dt), expand_ref[...], preferred_element_type=F32)
    acx_buf[...] = jnp.dot(_split_hi_lo(ac_buf[...]), expand_ref[...], preferred_element_type=F32)

    tpos = (seq_tile * ts + 1 + lax.broadcasted_iota(jnp.int32, (ts, LANES), 0)).astype(F32)
    for g, w in enumerate(POOL_WINDOWS):
        lanes = slice(g * POOL_GROUP, (g + 1) * POOL_GROUP)
        cur = pool_buf[POOL_HIST:POOL_HIST + ts, lanes]
        acc = cur
        for k in range(1, w):
            acc = acc + pool_buf[POOL_HIST - k:POOL_HIST - k + ts, lanes]
        p = acc / jnp.minimum(tpos, float(w)) - cur
        yp = jnp.dot(p.astype(BF16), wpool_ref[g], preferred_element_type=F32)
        ymix_buf[:, lanes] = (yp * pscale_ref[:, lanes]).astype(BF16)
    pool_buf[0:POOL_HIST, :] = pool_buf[ts:ts + POOL_HIST, :]

    first = CONV_HIST - (CONV_WIDTH - 1)
    for i in range(ts // ROW_BLOCK):
        r = i * ROW_BLOCK
        acc = convb_ref[...] + xbc_buf[first + r:first + r + ROW_BLOCK, :] * convw_ref[0:1, :]
        for k in range(1, CONV_WIDTH):
            acc = acc + xbc_buf[first + k + r:first + k + r + ROW_BLOCK, :] * convw_ref[k:k + 1, :]
        act = acc * jax.nn.sigmoid(acc)
        xs_buf[r:r + ROW_BLOCK, :] = act[:, 0:SSD_INNER]
        bc_buf[r:r + ROW_BLOCK, :] = act[:, SSD_INNER:].astype(BF16)
    xbc_buf[0:CONV_HIST, :] = xbc_buf[ts:ts + CONV_HIST, :]

    lane_id = lax.broadcasted_iota(jnp.int32, (CHUNK, LANES), 1)
    low_half = lane_id < SSD_HEAD_DIM
    bc_off = SSD_GROUPS * SSD_STATE

    def chunk_body(c, carry):
        r0 = pl.multiple_of(c * CHUNK, CHUNK)
        rows = pl.ds(r0, CHUNK)
        xs = xs_buf[rows, :]
        acx = acx_buf[rows, :]
        a_last = acx_buf[pl.ds(r0 + CHUNK - 1, 1), :]
        xdt = xs * dtx_buf[rows, :]
        xdt_bf = xdt.astype(BF16)
        xdec_bf = (xdt * jnp.exp(a_last - acx)).astype(BF16)
        e_cum = jnp.exp(acx)
        chunk_decay = jnp.exp(a_last)
        ac = ac_buf[rows, :]
        act = act_buf[pl.ds(pl.multiple_of(c * SSD_HEADS, SSD_HEADS), SSD_HEADS), :]
        y_groups = []
        for g in range(SSD_GROUPS):
            gl = slice(g * GROUP_WIDTH, (g + 1) * GROUP_WIDTH)
            b_g = bc_buf[rows, g * SSD_STATE:(g + 1) * SSD_STATE]
            c_g = bc_buf[rows, bc_off + g * SSD_STATE:bc_off + (g + 1) * SSD_STATE]
            scores = lax.dot_general(c_g, b_g, (((1,), (1,)), ((), ())),
                                     preferred_element_type=F32)
            pair_out = []
            for j in range(GROUP_WIDTH // LANES):
                m_pair = []
                for hh in range(2):
                    h = g * (SSD_HEADS // SSD_GROUPS) + 2 * j + hh
                    seg = ac[:, h:h + 1] - act[h:h + 1, :]
                    decay = jnp.exp(jnp.where(causal, seg, -jnp.inf))
                    m_pair.append((scores * decay).astype(BF16))
                lhs = jnp.concatenate(m_pair, axis=1)
                xp = xdt_bf[:, g * GROUP_WIDTH + j * LANES:g * GROUP_WIDTH + (j + 1) * LANES]
                zero = jnp.zeros_like(xp)
                rhs = jnp.concatenate([jnp.where(low_half, xp, zero),
                                       jnp.where(low_half, zero, xp)], axis=0)
                pair_out.append(jnp.dot(lhs, rhs, preferred_element_type=F32))
            y_diag = jnp.concatenate(pair_out, axis=1)
            prev = state_buf[g]
            y_off = jnp.dot(c_g, prev.astype(BF16), preferred_element_type=F32) * e_cum[:, gl]
            new_states = lax.dot_general(b_g, xdec_bf[:, gl], (((0,), (0,)), ((), ())),
                                         preferred_element_type=F32)
            state_buf[g] = prev * chunk_decay[:, gl] + new_states
            y_groups.append(y_diag + y_off)
        y = jnp.concatenate(y_groups, axis=1) + dskip_ref[...] * xs
        zc = z_buf[rows, :]
        y = y * (zc * jax.nn.sigmoid(zc))
        normed = []
        for g in range(SSD_GROUPS):
            yg = y[:, g * GROUP_WIDTH:(g + 1) * GROUP_WIDTH]
            ms = jnp.mean(yg * yg, axis=-1, keepdims=True)
            normed.append(yg * lax.rsqrt(ms + EPS))
        y = jnp.concatenate(normed, axis=1) * gssd_ref[...]
        ymix_buf[rows, POOL_WIDTH:MIX_WIDTH] = y.astype(BF16)
        return carry

    lax.fori_loop(0, n_chunks, chunk_body, 0)

    mixed = jnp.dot(ymix_buf[...], wout_ref[...], preferred_element_type=F32)
    o_ref[...] = x_ref[...] + mod_ref[2:3, :] * mixed


def _mixer(x, mod3, g_mix, w_in_p, conv_w, conv_b, dtb_p, alog_p, dskip_x, g_ssd, w_pool_b,
           pool_scale, w_out_b, expand):
    batch, seq, _ = x.shape
    ts = min(MIX_TILE, seq)
    tile = lambda b, s: (b, s, 0)
    in_specs = [
        pl.BlockSpec((None, ts, D_MODEL), tile),
        pl.BlockSpec((None, N_MOD, D_MODEL), lambda b, s: (b, 0, 0)),
        _const_spec((1, D_MODEL)),
        _const_spec((D_MODEL, IN_PAD)),
        _const_spec((CONV_WIDTH, CONV_CH)),
        _const_spec((1, CONV_CH)),
        _const_spec((1, LANES)),
        _const_spec((1, LANES)),
        _const_spec((1, SSD_INNER)),
        _const_spec((1, SSD_INNER)),
        _const_spec((len(POOL_WINDOWS), POOL_GROUP, POOL_GROUP)),
        _const_spec((1, POOL_WIDTH)),
        _const_spec((MIX_WIDTH, D_MODEL)),
        _const_spec((2 * LANES, SSD_INNER)),
    ]
    scratch = [
        pltpu.VMEM((ts, D_MODEL), BF16),
        pltpu.VMEM((POOL_HIST + ts, POOL_WIDTH), F32),
        pltpu.VMEM((ts, SSD_INNER), F32),
        pltpu.VMEM((CONV_HIST + ts, CONV_CH), F32),
        pltpu.VMEM((ts, SSD_INNER), F32),
        pltpu.VMEM((ts, 2 * SSD_GROUPS * SSD_STATE), BF16),
        pltpu.VMEM((ts, SSD_INNER), F32),
        pltpu.VMEM((ts, SSD_INNER), F32),
        pltpu.VMEM((ts, LANES), F32),
        pltpu.VMEM((ts // CHUNK * SSD_HEADS, CHUNK), F32),
        pltpu.VMEM((ts, MIX_WIDTH), BF16),
        pltpu.VMEM((SSD_GROUPS, SSD_STATE, GROUP_WIDTH), F32),
    ]
    return pl.pallas_call(
        _mixer_kernel,
        grid=(batch, seq // ts),
        in_specs=in_specs,
        out_specs=pl.BlockSpec((None, ts, D_MODEL), tile),
        out_shape=jax.ShapeDtypeStruct(x.shape, F32),
        scratch_shapes=scratch,
        compiler_params=pltpu.CompilerParams(
            dimension_semantics=("arbitrary", "arbitrary"), vmem_limit_bytes=VMEM_LIMIT),
        name="token_mixer",
    )(x, mod3, g_mix, w_in_p, conv_w, conv_b, dtb_p, alog_p, dskip_x, g_ssd, w_pool_b,
      pool_scale, w_out_b, expand)


def _mlp_kernel(h_ref, mod_ref, gmlp_ref, wup_ref, wdown_ref, gfin_ref, o_ref, u_buf, f_buf, acc_buf):
    tm = h_ref.shape[0]
    gain = gmlp_ref[...] * (1.0 + mod_ref[4:5, :])
    shift = mod_ref[3:4, :]
    gate = mod_ref[5:6, :]

    def norm_body(i, carry):
        r = pl.multiple_of(i * ROW_BLOCK, ROW_BLOCK)
        hb = h_ref[pl.ds(r, ROW_BLOCK), :]
        ms = jnp.mean(hb * hb, axis=-1, keepdims=True)
        u_buf[pl.ds(r, ROW_BLOCK), :] = (hb * lax.rsqrt(ms + EPS) * gain + shift).astype(BF16)
        return carry

    lax.fori_loop(0, tm // ROW_BLOCK, norm_body, 0)

    for j in range(D_FF // FF_CHUNK):
        cols = slice(j * FF_CHUNK, (j + 1) * FF_CHUNK)
        f = jnp.dot(u_buf[...], wup_ref[:, cols], preferred_element_type=F32)
        f_buf[...] = jnp.square(jnp.maximum(f, 0.0)).astype(BF16)
        part = jnp.dot(f_buf[...], wdown_ref[cols, :], preferred_element_type=F32)
        if j == 0:
            acc_buf[...] = part
        else:
            acc_buf[...] += part

    def out_body(i, carry):
        r = pl.multiple_of(i * ROW_BLOCK, ROW_BLOCK)
        hb = h_ref[pl.ds(r, ROW_BLOCK), :] + gate * acc_buf[pl.ds(r, ROW_BLOCK), :]
        ms = jnp.mean(hb * hb, axis=-1, keepdims=True)
        o_ref[pl.ds(r, ROW_BLOCK), :] = hb * lax.rsqrt(ms + EPS) * gfin_ref[...]
        return carry

    lax.fori_loop(0, tm // ROW_BLOCK, out_body, 0)


def _mlp(h, mod3, g_mlp, w_up_b, w_down_b, g_final):
    batch, seq, _ = h.shape
    tm = min(MLP_TILE, seq)
    tile = lambda b, s: (b, s, 0)
    return pl.pallas_call(
        _mlp_kernel,
        grid=(batch, seq // tm),
        in_specs=[
            pl.BlockSpec((None, tm, D_MODEL), tile),
            pl.BlockSpec((None, N_MOD, D_MODEL), lambda b, s: (b, 0, 0)),
            _const_spec((1, D_MODEL)),
            _const_spec((D_MODEL, D_FF)),
            _const_spec((D_FF, D_MODEL)),
            _const_spec((1, D_MODEL)),
        ],
        out_specs=pl.BlockSpec((None, tm, D_MODEL), tile),
        out_shape=jax.ShapeDtypeStruct(h.shape, F32),
        scratch_shapes=[
            pltpu.VMEM((tm, D_MODEL), BF16),
            pltpu.VMEM((tm, FF_CHUNK), BF16),
            pltpu.VMEM((tm, D_MODEL), F32),
        ],
        compiler_params=pltpu.CompilerParams(
            dimension_semantics=("arbitrary", "arbitrary"), vmem_limit_bytes=VMEM_LIMIT),
        name="relu2_mlp",
    )(h, mod3, g_mlp, w_up_b, w_down_b, g_final)


def _expand_matrix():
    head_of_lane = jnp.arange(SSD_INNER, dtype=jnp.int32) // SSD_HEAD_DIM
    row = jnp.arange(2 * LANES, dtype=jnp.int32) % LANES
    return (row[:, None] == head_of_lane[None, :]).astype(BF16)


def _pad_lanes(v):
    return jnp.pad(v.reshape(1, -1), ((0, 0), (0, LANES - v.shape[-1])))


def kernel(x, c, w_ada, b_ada, g_mix, w_in, conv_w, conv_b, dt_bias, a_log, d_skip, g_ssd, w_pool,
           pool_scale, w_out, g_mlp, w_up, w_down, g_final):
    batch = x.shape[0]
    assert w_ada.shape[0] == 1, "the MLP call fuses the final norm, so exactly one layer"
    layer = 0
    mod3 = _modulation(c, w_ada[layer], b_ada[layer]).reshape(batch, N_MOD, D_MODEL)
    w_in_p = jnp.pad(w_in[layer], ((0, 0), (0, IN_PAD - w_in.shape[-1]))).astype(BF16)
    h = _mixer(
        x, mod3, g_mix[layer].reshape(1, -1), w_in_p, conv_w[layer],
        conv_b[layer].reshape(1, -1), _pad_lanes(dt_bias[layer]), _pad_lanes(a_log[layer]),
        jnp.repeat(d_skip[layer], SSD_HEAD_DIM).reshape(1, -1), g_ssd[layer].reshape(1, -1),
        w_pool[layer].astype(BF16), pool_scale[layer].reshape(1, -1),
        w_out[layer].astype(BF16), _expand_matrix())
    return _mlp(h, mod3, g_mlp[layer].reshape(1, -1), w_up[layer].astype(BF16),
                w_down[layer].astype(BF16), g_final.reshape(1, -1))
```

```python
import jax
import jax.numpy as jnp
from jax import lax
from jax.experimental import pallas as pl
from jax.experimental.pallas import tpu as pltpu

F32 = jnp.float32
BF16 = jnp.bfloat16

D_MODEL = 1024
POOL_WIDTH = 512
POOL_WINDOWS = (2, 4, 8, 16)
POOL_GROUP = 128
SSD_INNER = 1024
SSD_HEADS = 16
SSD_HEAD_DIM = 64
SSD_GROUPS = 2
SSD_STATE = 128
GROUP_WIDTH = SSD_INNER // SSD_GROUPS
CONV_WIDTH = 4
CHUNK = 128
CONV_CH = SSD_INNER + 2 * SSD_GROUPS * SSD_STATE
MIX_WIDTH = POOL_WIDTH + SSD_INNER
OFF_Z = POOL_WIDTH
OFF_XBC = OFF_Z + SSD_INNER
OFF_DT = OFF_XBC + CONV_CH
LANES = 128
IN_PAD = OFF_DT + LANES
D_FF = 4096
N_MOD = 6
EPS = 1e-5

POOL_HIST = 16
CONV_HIST = 8
MIX_TILE = 512
MLP_TILE = 1024
FF_CHUNK = 1024
ROW_BLOCK = 64
TOKEN_PHASES = 4
LOG2_E = 1.4426950408889634
VMEM_LIMIT = 56 * 1024 * 1024


def _const_spec(shape):
    zeros = (0,) * len(shape)
    return pl.BlockSpec(shape, lambda *_: zeros, pipeline_mode=pl.Buffered(1))


def _mod_kernel(c_ref, w_ref, b_ref, o_ref):
    c = c_ref[...]
    c_act = (c * jax.nn.sigmoid(c)).astype(BF16)
    o_ref[...] = jnp.dot(c_act, w_ref[...].astype(BF16), preferred_element_type=F32) + b_ref[...]


def _modulation(c, w_ada, b_ada):
    batch = c.shape[0]
    n_out = w_ada.shape[1]
    return pl.pallas_call(
        _mod_kernel,
        grid=(n_out // D_MODEL,),
        in_specs=[
            pl.BlockSpec((batch, D_MODEL), lambda j: (0, 0)),
            pl.BlockSpec((D_MODEL, D_MODEL), lambda j: (0, j)),
            pl.BlockSpec((1, D_MODEL), lambda j: (0, j)),
        ],
        out_specs=pl.BlockSpec((batch, D_MODEL), lambda j: (0, j)),
        out_shape=jax.ShapeDtypeStruct((batch, n_out), F32),
        name="adaln_mod",
    )(c, w_ada, b_ada.reshape(1, n_out))


def _split_hi_lo(v):
    hi = v.astype(BF16)
    lo = (v - hi.astype(F32)).astype(BF16)
    return jnp.concatenate([hi, lo], axis=1)


def _silu(v):
    half = 0.5 * v
    return half + half * jnp.tanh(half)


def _phase_rows(start, n_rows):
    return pl.ds(start, n_rows // TOKEN_PHASES, stride=TOKEN_PHASES)


def _mixer_kernel(x_ref, mod_ref, gmix_ref, win_ref, convw_ref, convb_ref, dtb_ref, alog_ref,
                  dskip_ref, gssd_ref, wpool_ref, pscale_ref, wout_ref, expand_ref, o_ref,
                  u_buf, pool_buf, p_buf, z_buf, xbc_buf, act_buf, dtx_buf, acx_buf, ac_buf,
                  act_t_buf, ymix_buf, state_buf):
    ts = x_ref.shape[0]
    n_chunks = ts // CHUNK
    seq_tile = pl.program_id(1)
    n_pool = len(POOL_WINDOWS)
    n_xbc = CONV_CH // LANES
    n_xs = SSD_INNER // LANES

    @pl.when(seq_tile == 0)
    def _():
        pool_buf[:, 0:POOL_HIST, :] = jnp.zeros((n_pool, POOL_HIST, LANES), F32)
        xbc_buf[:, 0:CONV_HIST, :] = jnp.zeros((n_xbc, CONV_HIST, LANES), F32)
        state_buf[...] = jnp.zeros(state_buf.shape, F32)

    gain = gmix_ref[...] * (1.0 + mod_ref[1:2, :])
    shift = mod_ref[0:1, :]
    for i in range(ts // ROW_BLOCK):
        rows = slice(i * ROW_BLOCK, (i + 1) * ROW_BLOCK)
        xb = x_ref[rows, :]
        ms = jnp.mean(xb * xb, axis=-1, keepdims=True)
        u_buf[rows, :] = (xb * lax.rsqrt(ms + EPS) * gain + shift).astype(BF16)

    u = u_buf[...]
    xbc = jnp.dot(u, win_ref[:, OFF_XBC:OFF_DT], preferred_element_type=F32)
    for j in range(n_xbc):
        xbc_buf[j, CONV_HIST:CONV_HIST + ts, :] = xbc[:, j * LANES:(j + 1) * LANES]
    dt_raw = jnp.dot(u, win_ref[:, OFF_DT:IN_PAD], preferred_element_type=F32)
    pooled = jnp.dot(u, win_ref[:, 0:OFF_Z], preferred_element_type=F32)
    for g in range(n_pool):
        pool_buf[g, POOL_HIST:POOL_HIST + ts, :] = pooled[:, g * LANES:(g + 1) * LANES]
    z_buf[...] = jnp.dot(u, win_ref[:, OFF_Z:OFF_XBC], preferred_element_type=F32)

    dt = jax.nn.softplus(dt_raw + dtb_ref[...])
    da = dt * (-LOG2_E * jnp.exp(alog_ref[...]))
    row_id = lax.broadcasted_iota(jnp.int32, (CHUNK, CHUNK), 0)
    col_id = lax.broadcasted_iota(jnp.int32, (CHUNK, CHUNK), 1)
    causal = row_id >= col_id
    tril = causal.astype(F32)
    for c in range(n_chunks):
        rows = slice(c * CHUNK, (c + 1) * CHUNK)
        ac = jnp.dot(tril, da[rows], precision=lax.Precision.HIGHEST,
                     preferred_element_type=F32)
        ac_buf[rows, :] = ac
        act_t_buf[c * SSD_HEADS:(c + 1) * SSD_HEADS, :] = ac.T[0:SSD_HEADS, :]
    dtx_buf[...] = jnp.dot(_split_hi_lo(dt), expand_ref[...], preferred_element_type=F32)
    acx_buf[...] = jnp.dot(_split_hi_lo(ac_buf[...]), expand_ref[...], preferred_element_type=F32)

    phase_rows = ts // TOKEN_PHASES
    token0 = seq_tile * ts + 1 + TOKEN_PHASES * lax.broadcasted_iota(jnp.int32, (phase_rows, LANES), 0)
    tpos = [(token0 + s).astype(F32) for s in range(TOKEN_PHASES)]
    for g, w in enumerate(POOL_WINDOWS):
        lanes = slice(g * POOL_GROUP, (g + 1) * POOL_GROUP)
        view = {m: pool_buf[g, _phase_rows(POOL_HIST + m, ts), :]
                for m in range(1 - w, TOKEN_PHASES)}
        acc = view[0]
        for k in range(1, w):
            acc = acc + view[-k]
        for s in range(TOKEN_PHASES):
            if s > 0:
                acc = acc + view[s] - view[s - w]
            p_buf[g, _phase_rows(s, ts), :] = acc / jnp.minimum(tpos[s], float(w)) - view[s]
        yp = jnp.dot(p_buf[g].astype(BF16), wpool_ref[g], preferred_element_type=F32)
        ymix_buf[:, lanes] = (yp * pscale_ref[:, lanes]).astype(BF16)
        pool_buf[g, 0:POOL_HIST, :] = pool_buf[g, ts:ts + POOL_HIST, :]

    for j in range(n_xbc):
        lanes = slice(j * LANES, (j + 1) * LANES)
        view = [xbc_buf[j, _phase_rows(CONV_HIST + m, ts), :]
                for m in range(1 - CONV_WIDTH, TOKEN_PHASES)]
        taps = [convw_ref[k:k + 1, lanes] for k in range(CONV_WIDTH)]
        for s in range(TOKEN_PHASES):
            acc = convb_ref[:, lanes] + view[s] * taps[0]
            for k in range(1, CONV_WIDTH):
                acc = acc + view[s + k] * taps[k]
            act_buf[j, _phase_rows(s, ts), :] = _silu(acc)
        xbc_buf[j, 0:CONV_HIST, :] = xbc_buf[j, ts:ts + CONV_HIST, :]

    lane_id = lax.broadcasted_iota(jnp.int32, (CHUNK, LANES), 1)
    low_half = lane_id < SSD_HEAD_DIM
    slabs_per_group = GROUP_WIDTH // LANES
    heads_per_group = SSD_HEADS // SSD_GROUPS

    def chunk_body(c, carry):
        r0 = pl.multiple_of(c * CHUNK, CHUNK)
        rows = pl.ds(r0, CHUNK)
        ac = ac_buf[rows, :]
        ac_t = act_t_buf[pl.ds(pl.multiple_of(c * SSD_HEADS, SSD_HEADS), SSD_HEADS), :]
        for g in range(SSD_GROUPS):
            b_g = act_buf[n_xs + g, rows, :].astype(BF16)
            c_g = act_buf[n_xs + SSD_GROUPS + g, rows, :].astype(BF16)
            scores = lax.dot_general(c_g, b_g, (((1,), (1,)), ((), ())),
                                     preferred_element_type=F32)
            xs, acx, y_diag, xdec = [], [], [], []
            for j in range(slabs_per_group):
                slab = g * slabs_per_group + j
                lanes = slice(slab * LANES, (slab + 1) * LANES)
                xs_j = act_buf[slab, rows, :]
                acx_j = acx_buf[rows, lanes]
                a_last = acx_j[CHUNK - 1:CHUNK, :]
                xdt = xs_j * dtx_buf[rows, lanes]
                xdec.append((xdt * jnp.exp2(a_last - acx_j)).astype(BF16))
                m_pair = []
                for hh in range(2):
                    h = g * heads_per_group + 2 * j + hh
                    seg = ac[:, h:h + 1] - ac_t[h:h + 1, :]
                    decay = jnp.exp2(jnp.where(causal, seg, -jnp.inf))
                    m_pair.append((scores * decay).astype(BF16))
                lhs = jnp.concatenate(m_pair, axis=1)
                xp = xdt.astype(BF16)
                zero = jnp.zeros_like(xp)
                rhs = jnp.concatenate([jnp.where(low_half, xp, zero),
                                       jnp.where(low_half, zero, xp)], axis=0)
                y_diag.append(jnp.dot(lhs, rhs, preferred_element_type=F32))
                xs.append(xs_j)
                acx.append(acx_j)
            gl = slice(g * GROUP_WIDTH, (g + 1) * GROUP_WIDTH)
            prev = state_buf[g]
            y_off = jnp.dot(c_g, prev.astype(BF16), preferred_element_type=F32)
            new_states = lax.dot_general(b_g, jnp.concatenate(xdec, axis=1),
                                         (((0,), (0,)), ((), ())), preferred_element_type=F32)
            chunk_decay = jnp.exp2(jnp.concatenate([a[CHUNK - 1:CHUNK, :] for a in acx], axis=1))
            state_buf[g] = prev * chunk_decay + new_states
            gated, ssq = [], None
            for j in range(slabs_per_group):
                slab = g * slabs_per_group + j
                lanes = slice(slab * LANES, (slab + 1) * LANES)
                y = (y_diag[j] + y_off[:, j * LANES:(j + 1) * LANES] * jnp.exp2(acx[j])
                     + dskip_ref[:, lanes] * xs[j])
                y = y * _silu(z_buf[rows, lanes])
                gated.append(y)
                ssq = y * y if ssq is None else ssq + y * y
            inv = lax.rsqrt(jnp.sum(ssq, axis=-1, keepdims=True) * (1.0 / GROUP_WIDTH) + EPS)
            for j in range(slabs_per_group):
                slab = g * slabs_per_group + j
                lanes = slice(slab * LANES, (slab + 1) * LANES)
                ymix_buf[rows, POOL_WIDTH + slab * LANES:POOL_WIDTH + (slab + 1) * LANES] = (
                    gated[j] * inv * gssd_ref[:, lanes]).astype(BF16)
        return carry

    lax.fori_loop(0, n_chunks, chunk_body, 0)

    mixed = jnp.dot(ymix_buf[...], wout_ref[...], preferred_element_type=F32)
    o_ref[...] = x_ref[...] + mod_ref[2:3, :] * mixed


def _mixer(x, mod3, g_mix, w_in_p, conv_w, conv_b, dtb_p, alog_p, dskip_x, g_ssd, w_pool_b,
           pool_scale, w_out_b, expand):
    batch, seq, _ = x.shape
    ts = min(MIX_TILE, seq)
    tile = lambda b, s: (b, s, 0)
    in_specs = [
        pl.BlockSpec((None, ts, D_MODEL), tile),
        pl.BlockSpec((None, N_MOD, D_MODEL), lambda b, s: (b, 0, 0)),
        _const_spec((1, D_MODEL)),
        _const_spec((D_MODEL, IN_PAD)),
        _const_spec((CONV_WIDTH, CONV_CH)),
        _const_spec((1, CONV_CH)),
        _const_spec((1, LANES)),
        _const_spec((1, LANES)),
        _const_spec((1, SSD_INNER)),
        _const_spec((1, SSD_INNER)),
        _const_spec((len(POOL_WINDOWS), POOL_GROUP, POOL_GROUP)),
        _const_spec((1, POOL_WIDTH)),
        _const_spec((MIX_WIDTH, D_MODEL)),
        _const_spec((2 * LANES, SSD_INNER)),
    ]
    scratch = [
        pltpu.VMEM((ts, D_MODEL), BF16),
        pltpu.VMEM((len(POOL_WINDOWS), POOL_HIST + ts, LANES), F32),
        pltpu.VMEM((len(POOL_WINDOWS), ts, LANES), F32),
        pltpu.VMEM((ts, SSD_INNER), F32),
        pltpu.VMEM((CONV_CH // LANES, CONV_HIST + ts, LANES), F32),
        pltpu.VMEM((CONV_CH // LANES, ts, LANES), F32),
        pltpu.VMEM((ts, SSD_INNER), F32),
        pltpu.VMEM((ts, SSD_INNER), F32),
        pltpu.VMEM((ts, LANES), F32),
        pltpu.VMEM((ts // CHUNK * SSD_HEADS, CHUNK), F32),
        pltpu.VMEM((ts, MIX_WIDTH), BF16),
        pltpu.VMEM((SSD_GROUPS, SSD_STATE, GROUP_WIDTH), F32),
    ]
    return pl.pallas_call(
        _mixer_kernel,
        grid=(batch, seq // ts),
        in_specs=in_specs,
        out_specs=pl.BlockSpec((None, ts, D_MODEL), tile),
        out_shape=jax.ShapeDtypeStruct(x.shape, F32),
        scratch_shapes=scratch,
        compiler_params=pltpu.CompilerParams(
            dimension_semantics=("arbitrary", "arbitrary"), vmem_limit_bytes=VMEM_LIMIT),
        name="token_mixer",
    )(x, mod3, g_mix, w_in_p, conv_w, conv_b, dtb_p, alog_p, dskip_x, g_ssd, w_pool_b,
      pool_scale, w_out_b, expand)


def _mlp_kernel(h_ref, mod_ref, gmlp_ref, wup_ref, wdown_ref, gfin_ref, o_ref, u_buf, f_buf, acc_buf):
    tm = h_ref.shape[0]
    gain = gmlp_ref[...] * (1.0 + mod_ref[4:5, :])
    shift = mod_ref[3:4, :]
    gate = mod_ref[5:6, :]

    def norm_body(i, carry):
        r = pl.multiple_of(i * ROW_BLOCK, ROW_BLOCK)
        hb = h_ref[pl.ds(r, ROW_BLOCK), :]
        ms = jnp.mean(hb * hb, axis=-1, keepdims=True)
        u_buf[pl.ds(r, ROW_BLOCK), :] = (hb * lax.rsqrt(ms + EPS) * gain + shift).astype(BF16)
        return carry

    lax.fori_loop(0, tm // ROW_BLOCK, norm_body, 0)

    for j in range(D_FF // FF_CHUNK):
        cols = slice(j * FF_CHUNK, (j + 1) * FF_CHUNK)
        f = jnp.dot(u_buf[...], wup_ref[:, cols], preferred_element_type=F32)
        f_buf[...] = jnp.square(jnp.maximum(f, 0.0)).astype(BF16)
        part = jnp.dot(f_buf[...], wdown_ref[cols, :], preferred_element_type=F32)
        if j == 0:
            acc_buf[...] = part
        else:
            acc_buf[...] += part

    def out_body(i, carry):
        r = pl.multiple_of(i * ROW_BLOCK, ROW_BLOCK)
        hb = h_ref[pl.ds(r, ROW_BLOCK), :] + gate * acc_buf[pl.ds(r, ROW_BLOCK), :]
        ms = jnp.mean(hb * hb, axis=-1, keepdims=True)
        o_ref[pl.ds(r, ROW_BLOCK), :] = hb * lax.rsqrt(ms + EPS) * gfin_ref[...]
        return carry

    lax.fori_loop(0, tm // ROW_BLOCK, out_body, 0)


def _mlp(h, mod3, g_mlp, w_up_b, w_down_b, g_final):
    batch, seq, _ = h.shape
    tm = min(MLP_TILE, seq)
    tile = lambda b, s: (b, s, 0)
    return pl.pallas_call(
        _mlp_kernel,
        grid=(batch, seq // tm),
        in_specs=[
            pl.BlockSpec((None, tm, D_MODEL), tile),
            pl.BlockSpec((None, N_MOD, D_MODEL), lambda b, s: (b, 0, 0)),
            _const_spec((1, D_MODEL)),
            _const_spec((D_MODEL, D_FF)),
            _const_spec((D_FF, D_MODEL)),
            _const_spec((1, D_MODEL)),
        ],
        out_specs=pl.BlockSpec((None, tm, D_MODEL), tile),
        out_shape=jax.ShapeDtypeStruct(h.shape, F32),
        scratch_shapes=[
            pltpu.VMEM((tm, D_MODEL), BF16),
            pltpu.VMEM((tm, FF_CHUNK), BF16),
            pltpu.VMEM((tm, D_MODEL), F32),
        ],
        compiler_params=pltpu.CompilerParams(
            dimension_semantics=("arbitrary", "arbitrary"), vmem_limit_bytes=VMEM_LIMIT),
        name="relu2_mlp",
    )(h, mod3, g_mlp, w_up_b, w_down_b, g_final)


def _expand_matrix():
    head_of_lane = jnp.arange(SSD_INNER, dtype=jnp.int32) // SSD_HEAD_DIM
    row = jnp.arange(2 * LANES, dtype=jnp.int32) % LANES
    return (row[:, None] == head_of_lane[None, :]).astype(BF16)


def _pad_lanes(v):
    return jnp.pad(v.reshape(1, -1), ((0, 0), (0, LANES - v.shape[-1])))


def kernel(x, c, w_ada, b_ada, g_mix, w_in, conv_w, conv_b, dt_bias, a_log, d_skip, g_ssd, w_pool,
           pool_scale, w_out, g_mlp, w_up, w_down, g_final):
    batch = x.shape[0]
    assert w_ada.shape[0] == 1, "the MLP call fuses the final norm, so exactly one layer"
    layer = 0
    mod3 = _modulation(c, w_ada[layer], b_ada[layer]).reshape(batch, N_MOD, D_MODEL)
    w_in_p = jnp.pad(w_in[layer], ((0, 0), (0, IN_PAD - w_in.shape[-1]))).astype(BF16)
    h = _mixer(
        x, mod3, g_mix[layer].reshape(1, -1), w_in_p, conv_w[layer],
        conv_b[layer].reshape(1, -1), _pad_lanes(dt_bias[layer]), _pad_lanes(a_log[layer]),
        jnp.repeat(d_skip[layer], SSD_HEAD_DIM).reshape(1, -1), g_ssd[layer].reshape(1, -1),
        w_pool[layer].astype(BF16), pool_scale[layer].reshape(1, -1),
        w_out[layer].astype(BF16), _expand_matrix())
    return _mlp(h, mod3, g_mlp[layer].reshape(1, -1), w_up[layer].astype(BF16),
                w_down[layer].astype(BF16), g_final.reshape(1, -1))
```

```python
import functools

import jax
import jax.numpy as jnp
from jax import lax
from jax.experimental import pallas as pl
from jax.experimental.pallas import tpu as pltpu

F32 = jnp.float32
BF16 = jnp.bfloat16

D_MODEL = 1024
POOL_WIDTH = 512
POOL_WINDOWS = (2, 4, 8, 16)
POOL_GROUP = 128
SSD_INNER = 1024
SSD_HEADS = 16
SSD_HEAD_DIM = 64
SSD_GROUPS = 2
SSD_STATE = 128
GROUP_WIDTH = SSD_INNER // SSD_GROUPS
CONV_WIDTH = 4
CHUNK = 128
CONV_CH = SSD_INNER + 2 * SSD_GROUPS * SSD_STATE
MIX_WIDTH = POOL_WIDTH + SSD_INNER
OFF_Z = POOL_WIDTH
OFF_XBC = OFF_Z + SSD_INNER
OFF_DT = OFF_XBC + CONV_CH
LANES = 128
IN_PAD = OFF_DT + LANES
D_FF = 4096
N_MOD = 6
EPS = 1e-5

N_POOL = len(POOL_WINDOWS)
N_XBC = CONV_CH // LANES
N_XS = SSD_INNER // LANES
POOL_HIST = 16
CONV_HIST = 8
MIX_TILE = 512
MLP_TILE = 1024
FF_CHUNK = 1024
ROW_BLOCK = 64
TOKEN_PHASES = 4
LOG2_E = 1.4426950408889634
VMEM_LIMIT = 56 * 1024 * 1024


def _const_spec(shape):
    zeros = (0,) * len(shape)
    return pl.BlockSpec(shape, lambda *_: zeros, pipeline_mode=pl.Buffered(1))


def _mod_kernel(c_ref, w_ref, b_ref, o_ref):
    c = c_ref[...]
    c_act = (c * jax.nn.sigmoid(c)).astype(BF16)
    o_ref[...] = jnp.dot(c_act, w_ref[...].astype(BF16), preferred_element_type=F32) + b_ref[...]


def _modulation(c, w_ada, b_ada):
    batch = c.shape[0]
    n_out = w_ada.shape[1]
    return pl.pallas_call(
        _mod_kernel,
        grid=(n_out // D_MODEL,),
        in_specs=[
            pl.BlockSpec((batch, D_MODEL), lambda j: (0, 0)),
            pl.BlockSpec((D_MODEL, D_MODEL), lambda j: (0, j)),
            pl.BlockSpec((1, D_MODEL), lambda j: (0, j)),
        ],
        out_specs=pl.BlockSpec((batch, D_MODEL), lambda j: (0, j)),
        out_shape=jax.ShapeDtypeStruct((batch, n_out), F32),
        name="adaln_mod",
    )(c, w_ada, b_ada.reshape(1, n_out))


def _split_hi_lo(v):
    hi = v.astype(BF16)
    lo = (v - hi.astype(F32)).astype(BF16)
    return jnp.concatenate([hi, lo], axis=1)


def _silu(v):
    half = 0.5 * v
    return half + half * jnp.tanh(half)


def _phase_rows(start, n_rows):
    return pl.ds(start, n_rows // TOKEN_PHASES, stride=TOKEN_PHASES)


def _project_norm(x_ref, mod_ref, gmix_ref, u_buf):
    ts = x_ref.shape[0]
    gain = gmix_ref[...] * (1.0 + mod_ref[1:2, :])
    shift = mod_ref[0:1, :]
    for i in range(ts // ROW_BLOCK):
        rows = slice(i * ROW_BLOCK, (i + 1) * ROW_BLOCK)
        xb = x_ref[rows, :]
        ms = jnp.mean(xb * xb, axis=-1, keepdims=True)
        u_buf[rows, :] = (xb * lax.rsqrt(ms + EPS) * gain + shift).astype(BF16)


def _project(win_ref, u_buf, slot):
    pool_buf, xbc_buf, z_buf, dt_buf = slot
    ts = u_buf.shape[0]
    u = u_buf[...]
    xbc = jnp.dot(u, win_ref[:, OFF_XBC:OFF_DT], preferred_element_type=F32)
    for j in range(N_XBC):
        xbc_buf[j, CONV_HIST:CONV_HIST + ts, :] = xbc[:, j * LANES:(j + 1) * LANES]
    dt_buf[...] = jnp.dot(u, win_ref[:, OFF_DT:IN_PAD], preferred_element_type=F32)
    pooled = jnp.dot(u, win_ref[:, 0:OFF_Z], preferred_element_type=F32)
    for g in range(N_POOL):
        pool_buf[g, POOL_HIST:POOL_HIST + ts, :] = pooled[:, g * LANES:(g + 1) * LANES]
    z_buf[...] = jnp.dot(u, win_ref[:, OFF_Z:OFF_XBC], preferred_element_type=F32)


def _mix_prepare(seq_pos, next_starts_seq, slot, next_slot, convw_ref, convb_ref, dtb_ref, alog_ref,
                 wpool_ref, pscale_ref, expand_ref, p_buf, act_buf, dtx_buf, acx_buf, ac_buf,
                 act_t_buf, ymix_buf):
    pool_buf, xbc_buf, _, dt_buf = slot
    next_pool_buf, next_xbc_buf, _, _ = next_slot
    ts = dt_buf.shape[0]
    n_chunks = ts // CHUNK

    dt = jax.nn.softplus(dt_buf[...] + dtb_ref[...])
    da = dt * (-LOG2_E * jnp.exp(alog_ref[...]))
    row_id = lax.broadcasted_iota(jnp.int32, (CHUNK, CHUNK), 0)
    col_id = lax.broadcasted_iota(jnp.int32, (CHUNK, CHUNK), 1)
    tril = (row_id >= col_id).astype(F32)
    for c in range(n_chunks):
        rows = slice(c * CHUNK, (c + 1) * CHUNK)
        ac = jnp.dot(tril, da[rows], precision=lax.Precision.HIGHEST,
                     preferred_element_type=F32)
        ac_buf[rows, :] = ac
        act_t_buf[c * SSD_HEADS:(c + 1) * SSD_HEADS, :] = ac.T[0:SSD_HEADS, :]
    dtx_buf[...] = jnp.dot(_split_hi_lo(dt), expand_ref[...], preferred_element_type=F32)
    acx_buf[...] = jnp.dot(_split_hi_lo(ac_buf[...]), expand_ref[...], preferred_element_type=F32)

    phase_rows = ts // TOKEN_PHASES
    token0 = seq_pos * ts + 1 + TOKEN_PHASES * lax.broadcasted_iota(jnp.int32, (phase_rows, LANES), 0)
    tpos = [(token0 + s).astype(F32) for s in range(TOKEN_PHASES)]
    for g, w in enumerate(POOL_WINDOWS):
        lanes = slice(g * POOL_GROUP, (g + 1) * POOL_GROUP)
        view = {m: pool_buf[g, _phase_rows(POOL_HIST + m, ts), :]
                for m in range(1 - w, TOKEN_PHASES)}
        acc = view[0]
        for k in range(1, w):
            acc = acc + view[-k]
        for s in range(TOKEN_PHASES):
            if s > 0:
                acc = acc + view[s] - view[s - w]
            p_buf[g, _phase_rows(s, ts), :] = acc / jnp.minimum(tpos[s], float(w)) - view[s]
        yp = jnp.dot(p_buf[g].astype(BF16), wpool_ref[g], preferred_element_type=F32)
        ymix_buf[:, lanes] = (yp * pscale_ref[:, lanes]).astype(BF16)
        tail = pool_buf[g, ts:ts + POOL_HIST, :]
        next_pool_buf[g, 0:POOL_HIST, :] = jnp.where(next_starts_seq, jnp.zeros_like(tail), tail)

    for j in range(N_XBC):
        lanes = slice(j * LANES, (j + 1) * LANES)
        view = [xbc_buf[j, _phase_rows(CONV_HIST + m, ts), :]
                for m in range(1 - CONV_WIDTH, TOKEN_PHASES)]
        taps = [convw_ref[k:k + 1, lanes] for k in range(CONV_WIDTH)]
        for s in range(TOKEN_PHASES):
            acc = convb_ref[:, lanes] + view[s] * taps[0]
            for k in range(1, CONV_WIDTH):
                acc = acc + view[s + k] * taps[k]
            act_buf[j, _phase_rows(s, ts), :] = _silu(acc)
        tail = xbc_buf[j, ts:ts + CONV_HIST, :]
        next_xbc_buf[j, 0:CONV_HIST, :] = jnp.where(next_starts_seq, jnp.zeros_like(tail), tail)


def _mix_chunk(c, starts_seq, z_buf, dskip_ref, gssd_ref, act_buf, dtx_buf, acx_buf, ac_buf,
               act_t_buf, ymix_buf, state_buf):
    rows = slice(c * CHUNK, (c + 1) * CHUNK)
    row_id = lax.broadcasted_iota(jnp.int32, (CHUNK, CHUNK), 0)
    col_id = lax.broadcasted_iota(jnp.int32, (CHUNK, CHUNK), 1)
    causal = row_id >= col_id
    low_half = col_id < SSD_HEAD_DIM
    slabs_per_group = GROUP_WIDTH // LANES
    heads_per_group = SSD_HEADS // SSD_GROUPS
    ac = ac_buf[rows, :]
    ac_t = act_t_buf[c * SSD_HEADS:(c + 1) * SSD_HEADS, :]
    for g in range(SSD_GROUPS):
        b_g = act_buf[N_XS + g, rows, :].astype(BF16)
        c_g = act_buf[N_XS + SSD_GROUPS + g, rows, :].astype(BF16)
        scores = lax.dot_general(c_g, b_g, (((1,), (1,)), ((), ())),
                                 preferred_element_type=F32)
        xs, acx, y_diag, xdec = [], [], [], []
        for j in range(slabs_per_group):
            slab = g * slabs_per_group + j
            lanes = slice(slab * LANES, (slab + 1) * LANES)
            xs_j = act_buf[slab, rows, :]
            acx_j = acx_buf[rows, lanes]
            a_last = acx_j[CHUNK - 1:CHUNK, :]
            xdt = xs_j * dtx_buf[rows, lanes]
            xdec.append((xdt * jnp.exp2(a_last - acx_j)).astype(BF16))
            m_pair = []
            for hh in range(2):
                h = g * heads_per_group + 2 * j + hh
                seg = ac[:, h:h + 1] - ac_t[h:h + 1, :]
                decay = jnp.exp2(jnp.where(causal, seg, -jnp.inf))
                m_pair.append((scores * decay).astype(BF16))
            lhs = jnp.concatenate(m_pair, axis=1)
            xp = xdt.astype(BF16)
            zero = jnp.zeros_like(xp)
            rhs = jnp.concatenate([jnp.where(low_half, xp, zero),
                                   jnp.where(low_half, zero, xp)], axis=0)
            y_diag.append(jnp.dot(lhs, rhs, preferred_element_type=F32))
            xs.append(xs_j)
            acx.append(acx_j)
        prev = state_buf[g]
        if c == 0:
            prev = jnp.where(starts_seq, jnp.zeros_like(prev), prev)
        y_off = jnp.dot(c_g, prev.astype(BF16), preferred_element_type=F32)
        new_states = lax.dot_general(b_g, jnp.concatenate(xdec, axis=1),
                                     (((0,), (0,)), ((), ())), preferred_element_type=F32)
        chunk_decay = jnp.exp2(jnp.concatenate([a[CHUNK - 1:CHUNK, :] for a in acx], axis=1))
        state_buf[g] = prev * chunk_decay + new_states
        gated, ssq = [], None
        for j in range(slabs_per_group):
            slab = g * slabs_per_group + j
            lanes = slice(slab * LANES, (slab + 1) * LANES)
            y = (y_diag[j] + y_off[:, j * LANES:(j + 1) * LANES] * jnp.exp2(acx[j])
                 + dskip_ref[:, lanes] * xs[j])
            y = y * _silu(z_buf[rows, lanes])
            gated.append(y)
            ssq = y * y if ssq is None else ssq + y * y
        inv = lax.rsqrt(jnp.sum(ssq, axis=-1, keepdims=True) * (1.0 / GROUP_WIDTH) + EPS)
        for j in range(slabs_per_group):
            slab = g * slabs_per_group + j
            lanes = slice(slab * LANES, (slab + 1) * LANES)
            ymix_buf[rows, POOL_WIDTH + slab * LANES:POOL_WIDTH + (slab + 1) * LANES] = (
                gated[j] * inv * gssd_ref[:, lanes]).astype(BF16)


def _mixer_kernel(n_tiles, tiles_per_seq,
                  x_ref, xlag_ref, mod_ref, modlag_ref, gmix_ref, win_ref, convw_ref, convb_ref,
                  dtb_ref, alog_ref, dskip_ref, gssd_ref, wpool_ref, pscale_ref, wout_ref,
                  expand_ref, o_ref,
                  u_buf, pool0, xbc0, z0, dt0, pool1, xbc1, z1, dt1, p_buf, act_buf, dtx_buf,
                  acx_buf, ac_buf, act_t_buf, ymix_buf, state_buf):
    ts = x_ref.shape[0]
    n = pl.program_id(0)
    slots = ((pool0, xbc0, z0, dt0), (pool1, xbc1, z1, dt1))
    tables = (p_buf, act_buf, dtx_buf, acx_buf, ac_buf, act_t_buf, ymix_buf)

    def mix(slot, next_slot, project_slot):
        seq_pos = lax.rem(n - 1, tiles_per_seq)
        starts_seq = seq_pos == 0
        next_starts_seq = seq_pos == tiles_per_seq - 1
        if project_slot is not None:
            _project_norm(x_ref, mod_ref, gmix_ref, u_buf)
        _mix_prepare(seq_pos, next_starts_seq, slot, next_slot, convw_ref, convb_ref, dtb_ref,
                     alog_ref, wpool_ref, pscale_ref, expand_ref, *tables)
        if project_slot is not None:
            _project(win_ref, u_buf, project_slot)
        for c in range(ts // CHUNK):
            _mix_chunk(c, starts_seq, slot[2], dskip_ref, gssd_ref, act_buf, dtx_buf, acx_buf,
                       ac_buf, act_t_buf, ymix_buf, state_buf)
        mixed = jnp.dot(ymix_buf[...], wout_ref[...], preferred_element_type=F32)
        o_ref[...] = xlag_ref[...] + modlag_ref[2:3, :] * mixed

    @pl.when(n == 0)
    def _():
        pool0[:, 0:POOL_HIST, :] = jnp.zeros((N_POOL, POOL_HIST, LANES), F32)
        xbc0[:, 0:CONV_HIST, :] = jnp.zeros((N_XBC, CONV_HIST, LANES), F32)
        state_buf[...] = jnp.zeros(state_buf.shape, F32)
        _project_norm(x_ref, mod_ref, gmix_ref, u_buf)
        _project(win_ref, u_buf, slots[0])

    steady = (n > 0) & (n < n_tiles)

    @pl.when(steady & (lax.rem(n, 2) == 1))
    def _():
        mix(slots[0], slots[1], slots[1])

    @pl.when(steady & (lax.rem(n, 2) == 0))
    def _():
        mix(slots[1], slots[0], slots[0])

    @pl.when(n == n_tiles)
    def _():
        last = (n_tiles - 1) % 2
        mix(slots[last], slots[1 - last], None)


def _mixer(x, mod3, g_mix, w_in_p, conv_w, conv_b, dtb_p, alog_p, dskip_x, g_ssd, w_pool_b,
           pool_scale, w_out_b, expand):
    batch, seq, _ = x.shape
    ts = min(MIX_TILE, seq)
    tiles_per_seq = seq // ts
    n_tiles = batch * tiles_per_seq

    def current(n):
        t = jnp.minimum(n, n_tiles - 1)
        return t // tiles_per_seq, t % tiles_per_seq

    def lagged(n):
        t = jnp.maximum(n - 1, 0)
        return t // tiles_per_seq, t % tiles_per_seq

    in_specs = [
        pl.BlockSpec((None, ts, D_MODEL), lambda n: (*current(n), 0)),
        pl.BlockSpec((None, ts, D_MODEL), lambda n: (*lagged(n), 0)),
        pl.BlockSpec((None, N_MOD, D_MODEL), lambda n: (current(n)[0], 0, 0)),
        pl.BlockSpec((None, N_MOD, D_MODEL), lambda n: (lagged(n)[0], 0, 0)),
        _const_spec((1, D_MODEL)),
        _const_spec((D_MODEL, IN_PAD)),
        _const_spec((CONV_WIDTH, CONV_CH)),
        _const_spec((1, CONV_CH)),
        _const_spec((1, LANES)),
        _const_spec((1, LANES)),
        _const_spec((1, SSD_INNER)),
        _const_spec((1, SSD_INNER)),
        _const_spec((N_POOL, POOL_GROUP, POOL_GROUP)),
        _const_spec((1, POOL_WIDTH)),
        _const_spec((MIX_WIDTH, D_MODEL)),
        _const_spec((2 * LANES, SSD_INNER)),
    ]
    slot = [
        pltpu.VMEM((N_POOL, POOL_HIST + ts, LANES), F32),
        pltpu.VMEM((N_XBC, CONV_HIST + ts, LANES), F32),
        pltpu.VMEM((ts, SSD_INNER), F32),
        pltpu.VMEM((ts, LANES), F32),
    ]
    scratch = [pltpu.VMEM((ts, D_MODEL), BF16)] + slot + slot + [
        pltpu.VMEM((N_POOL, ts, LANES), F32),
        pltpu.VMEM((N_XBC, ts, LANES), F32),
        pltpu.VMEM((ts, SSD_INNER), F32),
        pltpu.VMEM((ts, SSD_INNER), F32),
        pltpu.VMEM((ts, LANES), F32),
        pltpu.VMEM((ts // CHUNK * SSD_HEADS, CHUNK), F32),
        pltpu.VMEM((ts, MIX_WIDTH), BF16),
        pltpu.VMEM((SSD_GROUPS, SSD_STATE, GROUP_WIDTH), F32),
    ]
    return pl.pallas_call(
        functools.partial(_mixer_kernel, n_tiles, tiles_per_seq),
        grid=(n_tiles + 1,),
        in_specs=in_specs,
        out_specs=pl.BlockSpec((None, ts, D_MODEL), lambda n: (*lagged(n), 0)),
        out_shape=jax.ShapeDtypeStruct(x.shape, F32),
        scratch_shapes=scratch,
        compiler_params=pltpu.CompilerParams(
            dimension_semantics=("arbitrary",), vmem_limit_bytes=VMEM_LIMIT),
        name="token_mixer",
    )(x, x, mod3, mod3, g_mix, w_in_p, conv_w, conv_b, dtb_p, alog_p, dskip_x, g_ssd, w_pool_b,
      pool_scale, w_out_b, expand)


def _mlp_kernel(h_ref, mod_ref, gmlp_ref, wup_ref, wdown_ref, gfin_ref, o_ref, u_buf, f_buf, acc_buf):
    tm = h_ref.shape[0]
    gain = gmlp_ref[...] * (1.0 + mod_ref[4:5, :])
    shift = mod_ref[3:4, :]
    gate = mod_ref[5:6, :]

    def norm_body(i, carry):
        r = pl.multiple_of(i * ROW_BLOCK, ROW_BLOCK)
        hb = h_ref[pl.ds(r, ROW_BLOCK), :]
        ms = jnp.mean(hb * hb, axis=-1, keepdims=True)
        u_buf[pl.ds(r, ROW_BLOCK), :] = (hb * lax.rsqrt(ms + EPS) * gain + shift).astype(BF16)
        return carry

    lax.fori_loop(0, tm // ROW_BLOCK, norm_body, 0)

    for j in range(D_FF // FF_CHUNK):
        cols = slice(j * FF_CHUNK, (j + 1) * FF_CHUNK)
        f = jnp.dot(u_buf[...], wup_ref[:, cols], preferred_element_type=F32)
        f_buf[...] = jnp.square(jnp.maximum(f, 0.0)).astype(BF16)
        part = jnp.dot(f_buf[...], wdown_ref[cols, :], preferred_element_type=F32)
        if j == 0:
            acc_buf[...] = part
        else:
            acc_buf[...] += part

    def out_body(i, carry):
        r = pl.multiple_of(i * ROW_BLOCK, ROW_BLOCK)
        hb = h_ref[pl.ds(r, ROW_BLOCK), :] + gate * acc_buf[pl.ds(r, ROW_BLOCK), :]
        ms = jnp.mean(hb * hb, axis=-1, keepdims=True)
        o_ref[pl.ds(r, ROW_BLOCK), :] = hb * lax.rsqrt(ms + EPS) * gfin_ref[...]
        return carry

    lax.fori_loop(0, tm // ROW_BLOCK, out_body, 0)


def _mlp(h, mod3, g_mlp, w_up_b, w_down_b, g_final):
    batch, seq, _ = h.shape
    tm = min(MLP_TILE, seq)
    tile = lambda b, s: (b, s, 0)
    return pl.pallas_call(
        _mlp_kernel,
        grid=(batch, seq // tm),
        in_specs=[
            pl.BlockSpec((None, tm, D_MODEL), tile),
            pl.BlockSpec((None, N_MOD, D_MODEL), lambda b, s: (b, 0, 0)),
            _const_spec((1, D_MODEL)),
            _const_spec((D_MODEL, D_FF)),
            _const_spec((D_FF, D_MODEL)),
            _const_spec((1, D_MODEL)),
        ],
        out_specs=pl.BlockSpec((None, tm, D_MODEL), tile),
        out_shape=jax.ShapeDtypeStruct(h.shape, F32),
        scratch_shapes=[
            pltpu.VMEM((tm, D_MODEL), BF16),
            pltpu.VMEM((tm, FF_CHUNK), BF16),
            pltpu.VMEM((tm, D_MODEL), F32),
        ],
        compiler_params=pltpu.CompilerParams(
            dimension_semantics=("arbitrary", "arbitrary"), vmem_limit_bytes=VMEM_LIMIT),
        name="relu2_mlp",
    )(h, mod3, g_mlp, w_up_b, w_down_b, g_final)


def _expand_matrix():
    head_of_lane = jnp.arange(SSD_INNER, dtype=jnp.int32) // SSD_HEAD_DIM
    row = jnp.arange(2 * LANES, dtype=jnp.int32) % LANES
    return (row[:, None] == head_of_lane[None, :]).astype(BF16)


def _pad_lanes(v):
    return jnp.pad(v.reshape(1, -1), ((0, 0), (0, LANES - v.shape[-1])))


def kernel(x, c, w_ada, b_ada, g_mix, w_in, conv_w, conv_b, dt_bias, a_log, d_skip, g_ssd, w_pool,
           pool_scale, w_out, g_mlp, w_up, w_down, g_final):
    batch = x.shape[0]
    assert w_ada.shape[0] == 1, "the MLP call fuses the final norm, so exactly one layer"
    layer = 0
    mod3 = _modulation(c, w_ada[layer], b_ada[layer]).reshape(batch, N_MOD, D_MODEL)
    w_in_p = jnp.pad(w_in[layer], ((0, 0), (0, IN_PAD - w_in.shape[-1]))).astype(BF16)
    h = _mixer(
        x, mod3, g_mix[layer].reshape(1, -1), w_in_p, conv_w[layer],
        conv_b[layer].reshape(1, -1), _pad_lanes(dt_bias[layer]), _pad_lanes(a_log[layer]),
        jnp.repeat(d_skip[layer], SSD_HEAD_DIM).reshape(1, -1), g_ssd[layer].reshape(1, -1),
        w_pool[layer].astype(BF16), pool_scale[layer].reshape(1, -1),
        w_out[layer].astype(BF16), _expand_matrix())
    return _mlp(h, mod3, g_mlp[layer].reshape(1, -1), w_up[layer].astype(BF16),
                w_down[layer].astype(BF16), g_final.reshape(1, -1))
```

```python
import functools

import jax
import jax.numpy as jnp
from jax import lax
from jax.experimental import pallas as pl
from jax.experimental.pallas import tpu as pltpu

F32 = jnp.float32
BF16 = jnp.bfloat16

D_MODEL = 1024
POOL_WIDTH = 512
POOL_WINDOWS = (2, 4, 8, 16)
POOL_GROUP = 128
SSD_INNER = 1024
SSD_HEADS = 16
SSD_HEAD_DIM = 64
SSD_GROUPS = 2
SSD_STATE = 128
GROUP_WIDTH = SSD_INNER // SSD_GROUPS
CONV_WIDTH = 4
CHUNK = 128
CONV_CH = SSD_INNER + 2 * SSD_GROUPS * SSD_STATE
MIX_WIDTH = POOL_WIDTH + SSD_INNER
OFF_Z = POOL_WIDTH
OFF_XBC = OFF_Z + SSD_INNER
OFF_DT = OFF_XBC + CONV_CH
LANES = 128
IN_PAD = OFF_DT + LANES
D_FF = 4096
N_MOD = 6
EPS = 1e-5

N_POOL = len(POOL_WINDOWS)
N_XBC = CONV_CH // LANES
N_XS = SSD_INNER // LANES
POOL_HIST = 16
CONV_HIST = 8
MIX_TILE = 512
MLP_TILE = 512
FF_CHUNK = 1024
ROW_BLOCK = 64
TOKEN_PHASES = 4
LOG2_E = 1.4426950408889634
VMEM_LIMIT = 56 * 1024 * 1024


def _const_spec(shape):
    zeros = (0,) * len(shape)
    return pl.BlockSpec(shape, lambda *_: zeros, pipeline_mode=pl.Buffered(1))


def _mod_kernel(c_ref, w_ref, b_ref, o_ref):
    c = c_ref[...]
    c_act = (c * jax.nn.sigmoid(c)).astype(BF16)
    o_ref[...] = jnp.dot(c_act, w_ref[...].astype(BF16), preferred_element_type=F32) + b_ref[...]


def _modulation(c, w_ada, b_ada):
    batch = c.shape[0]
    n_out = w_ada.shape[1]
    return pl.pallas_call(
        _mod_kernel,
        grid=(n_out // D_MODEL,),
        in_specs=[
            pl.BlockSpec((batch, D_MODEL), lambda j: (0, 0)),
            pl.BlockSpec((D_MODEL, D_MODEL), lambda j: (0, j)),
            pl.BlockSpec((1, D_MODEL), lambda j: (0, j)),
        ],
        out_specs=pl.BlockSpec((batch, D_MODEL), lambda j: (0, j)),
        out_shape=jax.ShapeDtypeStruct((batch, n_out), F32),
        name="adaln_mod",
    )(c, w_ada, b_ada.reshape(1, n_out))


def _split_hi_lo(v):
    hi = v.astype(BF16)
    lo = (v - hi.astype(F32)).astype(BF16)
    return jnp.concatenate([hi, lo], axis=1)


def _silu(v):
    half = 0.5 * v
    return half + half * jnp.tanh(half)


def _phase_rows(start, n_rows):
    return pl.ds(start, n_rows // TOKEN_PHASES, stride=TOKEN_PHASES)


def _project_norm(x_ref, mod_ref, gmix_ref, u_buf):
    ts = x_ref.shape[0]
    gain = gmix_ref[...] * (1.0 + mod_ref[1:2, :])
    shift = mod_ref[0:1, :]
    for i in range(ts // ROW_BLOCK):
        rows = slice(i * ROW_BLOCK, (i + 1) * ROW_BLOCK)
        xb = x_ref[rows, :]
        ms = jnp.mean(xb * xb, axis=-1, keepdims=True)
        u_buf[rows, :] = (xb * lax.rsqrt(ms + EPS) * gain + shift).astype(BF16)


def _project(win_ref, u_buf, slot):
    pool_buf, xbc_buf, z_buf, dt_buf = slot
    ts = u_buf.shape[0]
    u = u_buf[...]
    xbc = jnp.dot(u, win_ref[:, OFF_XBC:OFF_DT], preferred_element_type=F32)
    for j in range(N_XBC):
        xbc_buf[j, CONV_HIST:CONV_HIST + ts, :] = xbc[:, j * LANES:(j + 1) * LANES]
    dt_buf[...] = jnp.dot(u, win_ref[:, OFF_DT:IN_PAD], preferred_element_type=F32)
    pooled = jnp.dot(u, win_ref[:, 0:OFF_Z], preferred_element_type=F32)
    for g in range(N_POOL):
        pool_buf[g, POOL_HIST:POOL_HIST + ts, :] = pooled[:, g * LANES:(g + 1) * LANES]
    z_buf[...] = jnp.dot(u, win_ref[:, OFF_Z:OFF_XBC], preferred_element_type=F32)


def _mix_prepare(seq_pos, next_starts_seq, slot, next_slot, convw_ref, convb_ref, dtb_ref, alog_ref,
                 wpool_ref, pscale_ref, expand_ref, p_buf, act_buf, dtx_buf, acx_buf, ac_buf,
                 act_t_buf, ymix_buf):
    pool_buf, xbc_buf, _, dt_buf = slot
    next_pool_buf, next_xbc_buf, _, _ = next_slot
    ts = dt_buf.shape[0]
    n_chunks = ts // CHUNK

    dt = jax.nn.softplus(dt_buf[...] + dtb_ref[...])
    da = dt * (-LOG2_E * jnp.exp(alog_ref[...]))
    row_id = lax.broadcasted_iota(jnp.int32, (CHUNK, CHUNK), 0)
    col_id = lax.broadcasted_iota(jnp.int32, (CHUNK, CHUNK), 1)
    tril = (row_id >= col_id).astype(F32)
    for c in range(n_chunks):
        rows = slice(c * CHUNK, (c + 1) * CHUNK)
        ac = jnp.dot(tril, da[rows], precision=lax.Precision.HIGHEST,
                     preferred_element_type=F32)
        ac_buf[rows, :] = ac
        act_t_buf[c * SSD_HEADS:(c + 1) * SSD_HEADS, :] = ac.T[0:SSD_HEADS, :]
    dtx_buf[...] = jnp.dot(_split_hi_lo(dt), expand_ref[...], preferred_element_type=F32)
    acx_buf[...] = jnp.dot(_split_hi_lo(ac_buf[...]), expand_ref[...], preferred_element_type=F32)

    phase_rows = ts // TOKEN_PHASES
    token0 = seq_pos * ts + 1 + TOKEN_PHASES * lax.broadcasted_iota(jnp.int32, (phase_rows, LANES), 0)
    tpos = [(token0 + s).astype(F32) for s in range(TOKEN_PHASES)]
    for g, w in enumerate(POOL_WINDOWS):
        lanes = slice(g * POOL_GROUP, (g + 1) * POOL_GROUP)
        view = {m: pool_buf[g, _phase_rows(POOL_HIST + m, ts), :]
                for m in range(1 - w, TOKEN_PHASES)}
        acc = view[0]
        for k in range(1, w):
            acc = acc + view[-k]
        for s in range(TOKEN_PHASES):
            if s > 0:
                acc = acc + view[s] - view[s - w]
            p_buf[g, _phase_rows(s, ts), :] = acc / jnp.minimum(tpos[s], float(w)) - view[s]
        yp = jnp.dot(p_buf[g].astype(BF16), wpool_ref[g], preferred_element_type=F32)
        ymix_buf[:, lanes] = (yp * pscale_ref[:, lanes]).astype(BF16)
        tail = pool_buf[g, ts:ts + POOL_HIST, :]
        next_pool_buf[g, 0:POOL_HIST, :] = jnp.where(next_starts_seq, jnp.zeros_like(tail), tail)

    for j in range(N_XBC):
        lanes = slice(j * LANES, (j + 1) * LANES)
        view = [xbc_buf[j, _phase_rows(CONV_HIST + m, ts), :]
                for m in range(1 - CONV_WIDTH, TOKEN_PHASES)]
        taps = [convw_ref[k:k + 1, lanes] for k in range(CONV_WIDTH)]
        for s in range(TOKEN_PHASES):
            acc = convb_ref[:, lanes] + view[s] * taps[0]
            for k in range(1, CONV_WIDTH):
                acc = acc + view[s + k] * taps[k]
            act_buf[j, _phase_rows(s, ts), :] = _silu(acc)
        tail = xbc_buf[j, ts:ts + CONV_HIST, :]
        next_xbc_buf[j, 0:CONV_HIST, :] = jnp.where(next_starts_seq, jnp.zeros_like(tail), tail)


def _mix_chunk(c, starts_seq, z_buf, dskip_ref, gssd_ref, act_buf, dtx_buf, acx_buf, ac_buf,
               act_t_buf, ymix_buf, state_buf):
    rows = slice(c * CHUNK, (c + 1) * CHUNK)
    row_id = lax.broadcasted_iota(jnp.int32, (CHUNK, CHUNK), 0)
    col_id = lax.broadcasted_iota(jnp.int32, (CHUNK, CHUNK), 1)
    causal = row_id >= col_id
    low_half = col_id < SSD_HEAD_DIM
    slabs_per_group = GROUP_WIDTH // LANES
    heads_per_group = SSD_HEADS // SSD_GROUPS
    ac = ac_buf[rows, :]
    ac_t = act_t_buf[c * SSD_HEADS:(c + 1) * SSD_HEADS, :]
    for g in range(SSD_GROUPS):
        b_g = act_buf[N_XS + g, rows, :].astype(BF16)
        c_g = act_buf[N_XS + SSD_GROUPS + g, rows, :].astype(BF16)
        scores = lax.dot_general(c_g, b_g, (((1,), (1,)), ((), ())),
                                 preferred_element_type=F32)
        xs, acx, y_diag, xdec = [], [], [], []
        for j in range(slabs_per_group):
            slab = g * slabs_per_group + j
            lanes = slice(slab * LANES, (slab + 1) * LANES)
            xs_j = act_buf[slab, rows, :]
            acx_j = acx_buf[rows, lanes]
            a_last = acx_j[CHUNK - 1:CHUNK, :]
            xdt = xs_j * dtx_buf[rows, lanes]
            xdec.append((xdt * jnp.exp2(a_last - acx_j)).astype(BF16))
            m_pair = []
            for hh in range(2):
                h = g * heads_per_group + 2 * j + hh
                seg = ac[:, h:h + 1] - ac_t[h:h + 1, :]
                decay = jnp.exp2(jnp.where(causal, seg, -jnp.inf))
                m_pair.append((scores * decay).astype(BF16))
            lhs = jnp.concatenate(m_pair, axis=1)
            xp = xdt.astype(BF16)
            zero = jnp.zeros_like(xp)
            rhs = jnp.concatenate([jnp.where(low_half, xp, zero),
                                   jnp.where(low_half, zero, xp)], axis=0)
            y_diag.append(jnp.dot(lhs, rhs, preferred_element_type=F32))
            xs.append(xs_j)
            acx.append(acx_j)
        prev = state_buf[g]
        if c == 0:
            prev = jnp.where(starts_seq, jnp.zeros_like(prev), prev)
        y_off = jnp.dot(c_g, prev.astype(BF16), preferred_element_type=F32)
        new_states = lax.dot_general(b_g, jnp.concatenate(xdec, axis=1),
                                     (((0,), (0,)), ((), ())), preferred_element_type=F32)
        chunk_decay = jnp.exp2(jnp.concatenate([a[CHUNK - 1:CHUNK, :] for a in acx], axis=1))
        state_buf[g] = prev * chunk_decay + new_states
        gated, ssq = [], None
        for j in range(slabs_per_group):
            slab = g * slabs_per_group + j
            lanes = slice(slab * LANES, (slab + 1) * LANES)
            y = (y_diag[j] + y_off[:, j * LANES:(j + 1) * LANES] * jnp.exp2(acx[j])
                 + dskip_ref[:, lanes] * xs[j])
            y = y * _silu(z_buf[rows, lanes])
            gated.append(y)
            ssq = y * y if ssq is None else ssq + y * y
        inv = lax.rsqrt(jnp.sum(ssq, axis=-1, keepdims=True) * (1.0 / GROUP_WIDTH) + EPS)
        for j in range(slabs_per_group):
            slab = g * slabs_per_group + j
            lanes = slice(slab * LANES, (slab + 1) * LANES)
            ymix_buf[rows, POOL_WIDTH + slab * LANES:POOL_WIDTH + (slab + 1) * LANES] = (
                gated[j] * inv * gssd_ref[:, lanes]).astype(BF16)


def _mixer_kernel(n_tiles, tiles_per_seq,
                  x_ref, xlag_ref, mod_ref, modlag_ref, gmix_ref, win_ref, convw_ref, convb_ref,
                  dtb_ref, alog_ref, dskip_ref, gssd_ref, wpool_ref, pscale_ref, wout_ref,
                  expand_ref, o_ref,
                  u_buf, pool0, xbc0, z0, dt0, pool1, xbc1, z1, dt1, p_buf, act_buf, dtx_buf,
                  acx_buf, ac_buf, act_t_buf, ymix_buf, state_buf):
    ts = x_ref.shape[0]
    n = pl.program_id(0)
    slots = ((pool0, xbc0, z0, dt0), (pool1, xbc1, z1, dt1))
    tables = (p_buf, act_buf, dtx_buf, acx_buf, ac_buf, act_t_buf, ymix_buf)

    def mix(slot, next_slot, project_slot):
        seq_pos = lax.rem(n - 1, tiles_per_seq)
        starts_seq = seq_pos == 0
        next_starts_seq = seq_pos == tiles_per_seq - 1
        if project_slot is not None:
            _project_norm(x_ref, mod_ref, gmix_ref, u_buf)
        _mix_prepare(seq_pos, next_starts_seq, slot, next_slot, convw_ref, convb_ref, dtb_ref,
                     alog_ref, wpool_ref, pscale_ref, expand_ref, *tables)
        if project_slot is not None:
            _project(win_ref, u_buf, project_slot)
        for c in range(ts // CHUNK):
            _mix_chunk(c, starts_seq, slot[2], dskip_ref, gssd_ref, act_buf, dtx_buf, acx_buf,
                       ac_buf, act_t_buf, ymix_buf, state_buf)
        mixed = jnp.dot(ymix_buf[...], wout_ref[...], preferred_element_type=F32)
        o_ref[...] = xlag_ref[...] + modlag_ref[2:3, :] * mixed

    @pl.when(n == 0)
    def _():
        pool0[:, 0:POOL_HIST, :] = jnp.zeros((N_POOL, POOL_HIST, LANES), F32)
        xbc0[:, 0:CONV_HIST, :] = jnp.zeros((N_XBC, CONV_HIST, LANES), F32)
        state_buf[...] = jnp.zeros(state_buf.shape, F32)
        _project_norm(x_ref, mod_ref, gmix_ref, u_buf)
        _project(win_ref, u_buf, slots[0])

    steady = (n > 0) & (n < n_tiles)

    @pl.when(steady & (lax.rem(n, 2) == 1))
    def _():
        mix(slots[0], slots[1], slots[1])

    @pl.when(steady & (lax.rem(n, 2) == 0))
    def _():
        mix(slots[1], slots[0], slots[0])

    @pl.when(n == n_tiles)
    def _():
        last = (n_tiles - 1) % 2
        mix(slots[last], slots[1 - last], None)


def _mixer(x, mod3, g_mix, w_in_p, conv_w, conv_b, dtb_p, alog_p, dskip_x, g_ssd, w_pool_b,
           pool_scale, w_out_b, expand):
    batch, seq, _ = x.shape
    ts = min(MIX_TILE, seq)
    tiles_per_seq = seq // ts
    n_tiles = batch * tiles_per_seq

    def current(n):
        t = jnp.minimum(n, n_tiles - 1)
        return t // tiles_per_seq, t % tiles_per_seq

    def lagged(n):
        t = jnp.maximum(n - 1, 0)
        return t // tiles_per_seq, t % tiles_per_seq

    in_specs = [
        pl.BlockSpec((None, ts, D_MODEL), lambda n: (*current(n), 0)),
        pl.BlockSpec((None, ts, D_MODEL), lambda n: (*lagged(n), 0)),
        pl.BlockSpec((None, N_MOD, D_MODEL), lambda n: (current(n)[0], 0, 0)),
        pl.BlockSpec((None, N_MOD, D_MODEL), lambda n: (lagged(n)[0], 0, 0)),
        _const_spec((1, D_MODEL)),
        _const_spec((D_MODEL, IN_PAD)),
        _const_spec((CONV_WIDTH, CONV_CH)),
        _const_spec((1, CONV_CH)),
        _const_spec((1, LANES)),
        _const_spec((1, LANES)),
        _const_spec((1, SSD_INNER)),
        _const_spec((1, SSD_INNER)),
        _const_spec((N_POOL, POOL_GROUP, POOL_GROUP)),
        _const_spec((1, POOL_WIDTH)),
        _const_spec((MIX_WIDTH, D_MODEL)),
        _const_spec((2 * LANES, SSD_INNER)),
    ]
    slot = [
        pltpu.VMEM((N_POOL, POOL_HIST + ts, LANES), F32),
        pltpu.VMEM((N_XBC, CONV_HIST + ts, LANES), F32),
        pltpu.VMEM((ts, SSD_INNER), F32),
        pltpu.VMEM((ts, LANES), F32),
    ]
    scratch = [pltpu.VMEM((ts, D_MODEL), BF16)] + slot + slot + [
        pltpu.VMEM((N_POOL, ts, LANES), F32),
        pltpu.VMEM((N_XBC, ts, LANES), F32),
        pltpu.VMEM((ts, SSD_INNER), F32),
        pltpu.VMEM((ts, SSD_INNER), F32),
        pltpu.VMEM((ts, LANES), F32),
        pltpu.VMEM((ts // CHUNK * SSD_HEADS, CHUNK), F32),
        pltpu.VMEM((ts, MIX_WIDTH), BF16),
        pltpu.VMEM((SSD_GROUPS, SSD_STATE, GROUP_WIDTH), F32),
    ]
    return pl.pallas_call(
        functools.partial(_mixer_kernel, n_tiles, tiles_per_seq),
        grid=(n_tiles + 1,),
        in_specs=in_specs,
        out_specs=pl.BlockSpec((None, ts, D_MODEL), lambda n: (*lagged(n), 0)),
        out_shape=jax.ShapeDtypeStruct(x.shape, F32),
        scratch_shapes=scratch,
        compiler_params=pltpu.CompilerParams(
            dimension_semantics=("arbitrary",), vmem_limit_bytes=VMEM_LIMIT),
        name="token_mixer",
    )(x, x, mod3, mod3, g_mix, w_in_p, conv_w, conv_b, dtb_p, alog_p, dskip_x, g_ssd, w_pool_b,
      pool_scale, w_out_b, expand)


def _mlp_norm(h_ref, mod_ref, gmlp_ref, u_buf):
    gain = gmlp_ref[...] * (1.0 + mod_ref[4:5, :])
    shift = mod_ref[3:4, :]
    for i in range(h_ref.shape[0] // ROW_BLOCK):
        rows = slice(i * ROW_BLOCK, (i + 1) * ROW_BLOCK)
        hb = h_ref[rows, :]
        ms = jnp.mean(hb * hb, axis=-1, keepdims=True)
        u_buf[rows, :] = (hb * lax.rsqrt(ms + EPS) * gain + shift).astype(BF16)


def _mlp_matmuls(wup_ref, wdown_ref, u_buf, f_buf, acc_buf):
    for j in range(D_FF // FF_CHUNK):
        cols = slice(j * FF_CHUNK, (j + 1) * FF_CHUNK)
        f = jnp.dot(u_buf[...], wup_ref[:, cols], preferred_element_type=F32)
        f_buf[...] = jnp.square(jnp.maximum(f, 0.0)).astype(BF16)
        part = jnp.dot(f_buf[...], wdown_ref[cols, :], preferred_element_type=F32)
        if j == 0:
            acc_buf[...] = part
        else:
            acc_buf[...] += part


def _mlp_finish(h_ref, mod_ref, gfin_ref, acc_buf, o_ref):
    gate = mod_ref[5:6, :]
    for i in range(h_ref.shape[0] // ROW_BLOCK):
        rows = slice(i * ROW_BLOCK, (i + 1) * ROW_BLOCK)
        hb = h_ref[rows, :] + gate * acc_buf[rows, :]
        ms = jnp.mean(hb * hb, axis=-1, keepdims=True)
        o_ref[rows, :] = hb * lax.rsqrt(ms + EPS) * gfin_ref[...]


def _mlp_kernel(n_tiles, h_ref, hlag_ref, mod_ref, modlag_ref, gmlp_ref, wup_ref, wdown_ref,
                gfin_ref, o_ref, u0, u1, acc0, acc1, f_buf):
    n = pl.program_id(0)
    u_bufs = (u0, u1)
    acc_bufs = (acc0, acc1)

    def step(parity, norm, matmuls, finish):
        if finish:
            _mlp_finish(hlag_ref, modlag_ref, gfin_ref, acc_bufs[parity], o_ref)
        if norm:
            _mlp_norm(h_ref, mod_ref, gmlp_ref, u_bufs[parity])
        if matmuls:
            _mlp_matmuls(wup_ref, wdown_ref, u_bufs[1 - parity], f_buf, acc_bufs[1 - parity])

    @pl.when(n == 0)
    def _():
        step(0, True, False, False)

    @pl.when(n == 1)
    def _():
        step(1, True, True, False)

    steady = (n >= 2) & (n < n_tiles)

    @pl.when(steady & (lax.rem(n, 2) == 0))
    def _():
        step(0, True, True, True)

    @pl.when(steady & (lax.rem(n, 2) == 1))
    def _():
        step(1, True, True, True)

    @pl.when(n == n_tiles)
    def _():
        step(n_tiles % 2, False, True, True)

    @pl.when(n == n_tiles + 1)
    def _():
        step((n_tiles + 1) % 2, False, False, True)


def _mlp(h, mod3, g_mlp, w_up_b, w_down_b, g_final):
    batch, seq, _ = h.shape
    tm = min(MLP_TILE, seq)
    tiles_per_seq = seq // tm
    n_tiles = batch * tiles_per_seq
    assert n_tiles >= 2

    def current(n):
        t = jnp.minimum(n, n_tiles - 1)
        return t // tiles_per_seq, t % tiles_per_seq

    def lagged(n):
        t = jnp.maximum(n - 2, 0)
        return t // tiles_per_seq, t % tiles_per_seq

    return pl.pallas_call(
        functools.partial(_mlp_kernel, n_tiles),
        grid=(n_tiles + 2,),
        in_specs=[
            pl.BlockSpec((None, tm, D_MODEL), lambda n: (*current(n), 0)),
            pl.BlockSpec((None, tm, D_MODEL), lambda n: (*lagged(n), 0)),
            pl.BlockSpec((None, N_MOD, D_MODEL), lambda n: (current(n)[0], 0, 0)),
            pl.BlockSpec((None, N_MOD, D_MODEL), lambda n: (lagged(n)[0], 0, 0)),
            _const_spec((1, D_MODEL)),
            _const_spec((D_MODEL, D_FF)),
            _const_spec((D_FF, D_MODEL)),
            _const_spec((1, D_MODEL)),
        ],
        out_specs=pl.BlockSpec((None, tm, D_MODEL), lambda n: (*lagged(n), 0)),
        out_shape=jax.ShapeDtypeStruct(h.shape, F32),
        scratch_shapes=[
            pltpu.VMEM((tm, D_MODEL), BF16),
            pltpu.VMEM((tm, D_MODEL), BF16),
            pltpu.VMEM((tm, D_MODEL), F32),
            pltpu.VMEM((tm, D_MODEL), F32),
            pltpu.VMEM((tm, FF_CHUNK), BF16),
        ],
        compiler_params=pltpu.CompilerParams(
            dimension_semantics=("arbitrary",), vmem_limit_bytes=VMEM_LIMIT),
        name="relu2_mlp",
    )(h, h, mod3, mod3, g_mlp, w_up_b, w_down_b, g_final)


def _expand_matrix():
    head_of_lane = jnp.arange(SSD_INNER, dtype=jnp.int32) // SSD_HEAD_DIM
    row = jnp.arange(2 * LANES, dtype=jnp.int32) % LANES
    return (row[:, None] == head_of_lane[None, :]).astype(BF16)


def _pad_lanes(v):
    return jnp.pad(v.reshape(1, -1), ((0, 0), (0, LANES - v.shape[-1])))


def kernel(x, c, w_ada, b_ada, g_mix, w_in, conv_w, conv_b, dt_bias, a_log, d_skip, g_ssd, w_pool,
           pool_scale, w_out, g_mlp, w_up, w_down, g_final):
    batch = x.shape[0]
    assert w_ada.shape[0] == 1, "the MLP call fuses the final norm, so exactly one layer"
    layer = 0
    mod3 = _modulation(c, w_ada[layer], b_ada[layer]).reshape(batch, N_MOD, D_MODEL)
    w_in_p = jnp.pad(w_in[layer], ((0, 0), (0, IN_PAD - w_in.shape[-1]))).astype(BF16)
    h = _mixer(
        x, mod3, g_mix[layer].reshape(1, -1), w_in_p, conv_w[layer],
        conv_b[layer].reshape(1, -1), _pad_lanes(dt_bias[layer]), _pad_lanes(a_log[layer]),
        jnp.repeat(d_skip[layer], SSD_HEAD_DIM).reshape(1, -1), g_ssd[layer].reshape(1, -1),
        w_pool[layer].astype(BF16), pool_scale[layer].reshape(1, -1),
        w_out[layer].astype(BF16), _expand_matrix())
    return _mlp(h, mod3, g_mlp[layer].reshape(1, -1), w_up[layer].astype(BF16),
                w_down[layer].astype(BF16), g_final.reshape(1, -1))
```

```python
import functools

import jax
import jax.numpy as jnp
from jax import lax
from jax.experimental import pallas as pl
from jax.experimental.pallas import tpu as pltpu

F32 = jnp.float32
BF16 = jnp.bfloat16

D_MODEL = 1024
POOL_WIDTH = 512
POOL_WINDOWS = (2, 4, 8, 16)
POOL_GROUP = 128
SSD_INNER = 1024
SSD_HEADS = 16
SSD_HEAD_DIM = 64
SSD_GROUPS = 2
SSD_STATE = 128
GROUP_WIDTH = SSD_INNER // SSD_GROUPS
CONV_WIDTH = 4
CHUNK = 128
CONV_CH = SSD_INNER + 2 * SSD_GROUPS * SSD_STATE
MIX_WIDTH = POOL_WIDTH + SSD_INNER
OFF_Z = POOL_WIDTH
OFF_XBC = OFF_Z + SSD_INNER
OFF_DT = OFF_XBC + CONV_CH
LANES = 128
IN_PAD = OFF_DT + LANES
D_FF = 4096
N_MOD = 6
EPS = 1e-5

N_POOL = len(POOL_WINDOWS)
N_XBC = CONV_CH // LANES
N_XS = SSD_INNER // LANES
POOL_HIST = 16
CONV_HIST = 8
MIX_TILE = 512
MLP_TILE = 512
FF_CHUNK = 1024
ROW_BLOCK = 64
TOKEN_PHASES = 4
LOG2_E = 1.4426950408889634
VMEM_LIMIT = 56 * 1024 * 1024


def _const_spec(shape):
    zeros = (0,) * len(shape)
    return pl.BlockSpec(shape, lambda *_: zeros, pipeline_mode=pl.Buffered(1))


def _mod_kernel(c_ref, w_ref, b_ref, o_ref):
    c = c_ref[...]
    c_act = (c * jax.nn.sigmoid(c)).astype(BF16)
    o_ref[...] = jnp.dot(c_act, w_ref[...].astype(BF16), preferred_element_type=F32) + b_ref[...]


def _modulation(c, w_ada, b_ada):
    batch = c.shape[0]
    n_out = w_ada.shape[1]
    return pl.pallas_call(
        _mod_kernel,
        grid=(n_out // D_MODEL,),
        in_specs=[
            pl.BlockSpec((batch, D_MODEL), lambda j: (0, 0)),
            pl.BlockSpec((D_MODEL, D_MODEL), lambda j: (0, j)),
            pl.BlockSpec((1, D_MODEL), lambda j: (0, j)),
        ],
        out_specs=pl.BlockSpec((batch, D_MODEL), lambda j: (0, j)),
        out_shape=jax.ShapeDtypeStruct((batch, n_out), F32),
        name="adaln_mod",
    )(c, w_ada, b_ada.reshape(1, n_out))


def _silu(v):
    half = 0.5 * v
    return half + half * jnp.tanh(half)


def _phase_rows(start, n_rows):
    return pl.ds(start, n_rows // TOKEN_PHASES, stride=TOKEN_PHASES)


def _project_norm(x_ref, mod_ref, gmix_ref, u_buf):
    ts = x_ref.shape[0]
    gain = gmix_ref[...] * (1.0 + mod_ref[1:2, :])
    shift = mod_ref[0:1, :]
    for i in range(ts // ROW_BLOCK):
        rows = slice(i * ROW_BLOCK, (i + 1) * ROW_BLOCK)
        xb = x_ref[rows, :]
        ms = jnp.mean(xb * xb, axis=-1, keepdims=True)
        u_buf[rows, :] = (xb * lax.rsqrt(ms + EPS) * gain + shift).astype(BF16)


def _project(win_ref, u_buf, slot):
    pool_buf, xbc_buf, z_buf, dt_buf = slot
    ts = u_buf.shape[0]
    u = u_buf[...]
    xbc = jnp.dot(u, win_ref[:, OFF_XBC:OFF_DT], preferred_element_type=F32)
    for j in range(N_XBC):
        xbc_buf[j, CONV_HIST:CONV_HIST + ts, :] = xbc[:, j * LANES:(j + 1) * LANES]
    dt_buf[...] = jnp.dot(u, win_ref[:, OFF_DT:IN_PAD], preferred_element_type=F32)
    pooled = jnp.dot(u, win_ref[:, 0:OFF_Z], preferred_element_type=F32)
    for g in range(N_POOL):
        pool_buf[g, POOL_HIST:POOL_HIST + ts, :] = pooled[:, g * LANES:(g + 1) * LANES]
    z_buf[...] = jnp.dot(u, win_ref[:, OFF_Z:OFF_XBC], preferred_element_type=F32)


def _mix_prepare(seq_pos, next_starts_seq, slot, next_slot, convw_ref, convb_ref, dtb_ref, alog_ref,
                 wpool_ref, pscale_ref, p_buf, act_buf, ac_buf, w_buf, ac_t_buf, dt_t_buf, ymix_buf):
    pool_buf, xbc_buf, _, dt_buf = slot
    next_pool_buf, next_xbc_buf, _, _ = next_slot
    ts = dt_buf.shape[0]
    n_chunks = ts // CHUNK

    dt = jax.nn.softplus(dt_buf[...] + dtb_ref[...])
    da = dt * (-LOG2_E * jnp.exp(alog_ref[...]))
    row_id = lax.broadcasted_iota(jnp.int32, (CHUNK, CHUNK), 0)
    col_id = lax.broadcasted_iota(jnp.int32, (CHUNK, CHUNK), 1)
    tril = (row_id >= col_id).astype(BF16)
    tril3 = jnp.concatenate([tril, tril, tril], axis=1)
    for c in range(n_chunks):
        rows = slice(c * CHUNK, (c + 1) * CHUNK)
        hs = slice(c * SSD_HEADS, (c + 1) * SSD_HEADS)
        da_c = da[rows]
        hi = da_c.astype(BF16)
        rest = da_c - hi.astype(F32)
        mid = rest.astype(BF16)
        low = (rest - mid.astype(F32)).astype(BF16)
        ac = jnp.dot(tril3, jnp.concatenate([hi, mid, low], axis=0),
                     preferred_element_type=F32)
        dt_c = dt[rows]
        ac_buf[rows, :] = ac
        w_buf[rows, :] = dt_c * jnp.exp2(ac[CHUNK - 1:CHUNK, :] - ac)
        ac_t_buf[hs, :] = ac.T[0:SSD_HEADS, :]
        dt_t_buf[hs, :] = dt_c.T[0:SSD_HEADS, :]

    phase_rows = ts // TOKEN_PHASES
    token0 = seq_pos * ts + 1 + TOKEN_PHASES * lax.broadcasted_iota(jnp.int32, (phase_rows, LANES), 0)
    tpos = [(token0 + s).astype(F32) for s in range(TOKEN_PHASES)]
    for g, w in enumerate(POOL_WINDOWS):
        lanes = slice(g * POOL_GROUP, (g + 1) * POOL_GROUP)
        view = {m: pool_buf[g, _phase_rows(POOL_HIST + m, ts), :]
                for m in range(1 - w, TOKEN_PHASES)}
        acc = view[0]
        for k in range(1, w):
            acc = acc + view[-k]
        for s in range(TOKEN_PHASES):
            if s > 0:
                acc = acc + view[s] - view[s - w]
            p_buf[g, _phase_rows(s, ts), :] = acc / jnp.minimum(tpos[s], float(w)) - view[s]
        yp = jnp.dot(p_buf[g].astype(BF16), wpool_ref[g], preferred_element_type=F32)
        ymix_buf[:, lanes] = (yp * pscale_ref[:, lanes]).astype(BF16)
        tail = pool_buf[g, ts:ts + POOL_HIST, :]
        next_pool_buf[g, 0:POOL_HIST, :] = jnp.where(next_starts_seq, jnp.zeros_like(tail), tail)

    for j in range(N_XBC):
        lanes = slice(j * LANES, (j + 1) * LANES)
        view = [xbc_buf[j, _phase_rows(CONV_HIST + m, ts), :]
                for m in range(1 - CONV_WIDTH, TOKEN_PHASES)]
        taps = [convw_ref[k:k + 1, lanes] for k in range(CONV_WIDTH)]
        for s in range(TOKEN_PHASES):
            acc = convb_ref[:, lanes] + view[s] * taps[0]
            for k in range(1, CONV_WIDTH):
                acc = acc + view[s + k] * taps[k]
            act_buf[j, _phase_rows(s, ts), :] = _silu(acc)
        tail = xbc_buf[j, ts:ts + CONV_HIST, :]
        next_xbc_buf[j, 0:CONV_HIST, :] = jnp.where(next_starts_seq, jnp.zeros_like(tail), tail)


def _mix_chunk(c, starts_seq, z_buf, dskip_ref, gssd_ref, act_buf, ac_buf, w_buf, ac_t_buf,
               dt_t_buf, ymix_buf, state_buf, yoff_buf, y_buf):
    rows = slice(c * CHUNK, (c + 1) * CHUNK)
    row_id = lax.broadcasted_iota(jnp.int32, (CHUNK, CHUNK), 0)
    col_id = lax.broadcasted_iota(jnp.int32, (CHUNK, CHUNK), 1)
    causal = row_id >= col_id
    low_half = col_id < SSD_HEAD_DIM
    slabs_per_group = GROUP_WIDTH // LANES
    heads_per_group = SSD_HEADS // SSD_GROUPS
    ac = ac_buf[rows, :]
    w = w_buf[rows, :]
    ac_t = ac_t_buf[c * SSD_HEADS:(c + 1) * SSD_HEADS, :]
    dt_t = dt_t_buf[c * SSD_HEADS:(c + 1) * SSD_HEADS, :]

    def head_columns(table, h):
        return [jnp.broadcast_to(table[:, k:k + 1], (CHUNK, LANES)) for k in (h, h + 1)]

    def carried_state(g):
        prev = state_buf[g]
        if c == 0:
            prev = jnp.where(starts_seq, jnp.zeros_like(prev), prev)
        return prev

    for g in range(SSD_GROUPS):
        b_g = act_buf[N_XS + g, rows, :].astype(BF16)
        c_g = act_buf[N_XS + SSD_GROUPS + g, rows, :].astype(BF16)
        scores = lax.dot_general(c_g, b_g, (((1,), (1,)), ((), ())),
                                 preferred_element_type=F32)
        yoff_buf[g] = jnp.dot(c_g, carried_state(g).astype(BF16), preferred_element_type=F32)
        ssq, a_last, xdec = None, [], []
        for j in range(slabs_per_group):
            slab = g * slabs_per_group + j
            lanes = slice(slab * LANES, (slab + 1) * LANES)
            group_lanes = slice(j * LANES, (j + 1) * LANES)
            h0 = g * heads_per_group + 2 * j
            xs = act_buf[slab, rows, :]
            ac_cols = head_columns(ac, h0)
            w_cols = head_columns(w, h0)
            acx = jnp.where(low_half, ac_cols[0], ac_cols[1])
            a_last.append(acx[CHUNK - 1:CHUNK, :])
            xdec.append((xs * jnp.where(low_half, w_cols[0], w_cols[1])).astype(BF16))
            m_pair = []
            for k in range(2):
                h = h0 + k
                seg = ac_cols[k] - ac_t[h:h + 1, :]
                decay = jnp.exp2(jnp.where(causal, seg, -jnp.inf))
                m_pair.append((scores * decay * dt_t[h:h + 1, :]).astype(BF16))
            lhs = jnp.concatenate(m_pair, axis=1)
            xp = xs.astype(BF16)
            zero = jnp.zeros_like(xp)
            rhs = jnp.concatenate([jnp.where(low_half, xp, zero),
                                   jnp.where(low_half, zero, xp)], axis=0)
            y = (jnp.dot(lhs, rhs, preferred_element_type=F32)
                 + yoff_buf[g, :, group_lanes] * jnp.exp2(acx) + dskip_ref[:, lanes] * xs)
            y = y * _silu(z_buf[rows, lanes])
            y_buf[g, :, group_lanes] = y
            ssq = y * y if ssq is None else ssq + y * y
        new_states = lax.dot_general(b_g, jnp.concatenate(xdec, axis=1),
                                     (((0,), (0,)), ((), ())), preferred_element_type=F32)
        chunk_decay = jnp.exp2(jnp.concatenate(a_last, axis=1))
        state_buf[g] = carried_state(g) * chunk_decay + new_states
        inv = lax.rsqrt(jnp.sum(ssq, axis=-1, keepdims=True) * (1.0 / GROUP_WIDTH) + EPS)
        for j in range(slabs_per_group):
            slab = g * slabs_per_group + j
            lanes = slice(slab * LANES, (slab + 1) * LANES)
            ymix_buf[rows, POOL_WIDTH + slab * LANES:POOL_WIDTH + (slab + 1) * LANES] = (
                y_buf[g, :, j * LANES:(j + 1) * LANES] * inv * gssd_ref[:, lanes]).astype(BF16)


def _mixer_kernel(n_tiles, tiles_per_seq,
                  x_ref, xlag_ref, mod_ref, modlag_ref, gmix_ref, win_ref, convw_ref, convb_ref,
                  dtb_ref, alog_ref, dskip_ref, gssd_ref, wpool_ref, pscale_ref, wout_ref, o_ref,
                  u_buf, pool0, xbc0, z0, dt0, pool1, xbc1, z1, dt1, p_buf, act_buf, ac_buf,
                  w_buf, ac_t_buf, dt_t_buf, ymix_buf, state_buf, yoff_buf, y_buf):
    ts = x_ref.shape[0]
    n = pl.program_id(0)
    slots = ((pool0, xbc0, z0, dt0), (pool1, xbc1, z1, dt1))

    def mix(slot, next_slot, project_slot):
        seq_pos = lax.rem(n - 1, tiles_per_seq)
        starts_seq = seq_pos == 0
        next_starts_seq = seq_pos == tiles_per_seq - 1
        if project_slot is not None:
            _project_norm(x_ref, mod_ref, gmix_ref, u_buf)
        _mix_prepare(seq_pos, next_starts_seq, slot, next_slot, convw_ref, convb_ref, dtb_ref,
                     alog_ref, wpool_ref, pscale_ref, p_buf, act_buf, ac_buf, w_buf, ac_t_buf,
                     dt_t_buf, ymix_buf)
        if project_slot is not None:
            _project(win_ref, u_buf, project_slot)
        for c in range(ts // CHUNK):
            _mix_chunk(c, starts_seq, slot[2], dskip_ref, gssd_ref, act_buf, ac_buf, w_buf,
                       ac_t_buf, dt_t_buf, ymix_buf, state_buf, yoff_buf, y_buf)
        mixed = jnp.dot(ymix_buf[...], wout_ref[...], preferred_element_type=F32)
        o_ref[...] = xlag_ref[...] + modlag_ref[2:3, :] * mixed

    @pl.when(n == 0)
    def _():
        pool0[:, 0:POOL_HIST, :] = jnp.zeros((N_POOL, POOL_HIST, LANES), F32)
        xbc0[:, 0:CONV_HIST, :] = jnp.zeros((N_XBC, CONV_HIST, LANES), F32)
        state_buf[...] = jnp.zeros(state_buf.shape, F32)
        _project_norm(x_ref, mod_ref, gmix_ref, u_buf)
        _project(win_ref, u_buf, slots[0])

    steady = (n > 0) & (n < n_tiles)

    @pl.when(steady & (lax.rem(n, 2) == 1))
    def _():
        mix(slots[0], slots[1], slots[1])

    @pl.when(steady & (lax.rem(n, 2) == 0))
    def _():
        mix(slots[1], slots[0], slots[0])

    @pl.when(n == n_tiles)
    def _():
        last = (n_tiles - 1) % 2
        mix(slots[last], slots[1 - last], None)


def _mixer(x, mod3, g_mix, w_in_p, conv_w, conv_b, dtb_p, alog_p, dskip_x, g_ssd, w_pool_b,
           pool_scale, w_out_b):
    batch, seq, _ = x.shape
    ts = min(MIX_TILE, seq)
    tiles_per_seq = seq // ts
    n_tiles = batch * tiles_per_seq

    def current(n):
        t = jnp.minimum(n, n_tiles - 1)
        return t // tiles_per_seq, t % tiles_per_seq

    def lagged(n):
        t = jnp.maximum(n - 1, 0)
        return t // tiles_per_seq, t % tiles_per_seq

    in_specs = [
        pl.BlockSpec((None, ts, D_MODEL), lambda n: (*current(n), 0)),
        pl.BlockSpec((None, ts, D_MODEL), lambda n: (*lagged(n), 0)),
        pl.BlockSpec((None, N_MOD, D_MODEL), lambda n: (current(n)[0], 0, 0)),
        pl.BlockSpec((None, N_MOD, D_MODEL), lambda n: (lagged(n)[0], 0, 0)),
        _const_spec((1, D_MODEL)),
        _const_spec((D_MODEL, IN_PAD)),
        _const_spec((CONV_WIDTH, CONV_CH)),
        _const_spec((1, CONV_CH)),
        _const_spec((1, LANES)),
        _const_spec((1, LANES)),
        _const_spec((1, SSD_INNER)),
        _const_spec((1, SSD_INNER)),
        _const_spec((N_POOL, POOL_GROUP, POOL_GROUP)),
        _const_spec((1, POOL_WIDTH)),
        _const_spec((MIX_WIDTH, D_MODEL)),
    ]
    slot = [
        pltpu.VMEM((N_POOL, POOL_HIST + ts, LANES), F32),
        pltpu.VMEM((N_XBC, CONV_HIST + ts, LANES), F32),
        pltpu.VMEM((ts, SSD_INNER), F32),
        pltpu.VMEM((ts, LANES), F32),
    ]
    scratch = [pltpu.VMEM((ts, D_MODEL), BF16)] + slot + slot + [
        pltpu.VMEM((N_POOL, ts, LANES), F32),
        pltpu.VMEM((N_XBC, ts, LANES), F32),
        pltpu.VMEM((ts, LANES), F32),
        pltpu.VMEM((ts, LANES), F32),
        pltpu.VMEM((ts // CHUNK * SSD_HEADS, CHUNK), F32),
        pltpu.VMEM((ts // CHUNK * SSD_HEADS, CHUNK), F32),
        pltpu.VMEM((ts, MIX_WIDTH), BF16),
        pltpu.VMEM((SSD_GROUPS, SSD_STATE, GROUP_WIDTH), F32),
        pltpu.VMEM((SSD_GROUPS, CHUNK, GROUP_WIDTH), F32),
        pltpu.VMEM((SSD_GROUPS, CHUNK, GROUP_WIDTH), F32),
    ]
    return pl.pallas_call(
        functools.partial(_mixer_kernel, n_tiles, tiles_per_seq),
        grid=(n_tiles + 1,),
        in_specs=in_specs,
        out_specs=pl.BlockSpec((None, ts, D_MODEL), lambda n: (*lagged(n), 0)),
        out_shape=jax.ShapeDtypeStruct(x.shape, F32),
        scratch_shapes=scratch,
        compiler_params=pltpu.CompilerParams(
            dimension_semantics=("arbitrary",), vmem_limit_bytes=VMEM_LIMIT),
        name="token_mixer",
    )(x, x, mod3, mod3, g_mix, w_in_p, conv_w, conv_b, dtb_p, alog_p, dskip_x, g_ssd, w_pool_b,
      pool_scale, w_out_b)


def _mlp_norm(h_ref, mod_ref, gmlp_ref, u_buf):
    gain = gmlp_ref[...] * (1.0 + mod_ref[4:5, :])
    shift = mod_ref[3:4, :]
    for i in range(h_ref.shape[0] // ROW_BLOCK):
        rows = slice(i * ROW_BLOCK, (i + 1) * ROW_BLOCK)
        hb = h_ref[rows, :]
        ms = jnp.mean(hb * hb, axis=-1, keepdims=True)
        u_buf[rows, :] = (hb * lax.rsqrt(ms + EPS) * gain + shift).astype(BF16)


def _mlp_matmuls(wup_ref, wdown_ref, u_buf, f_buf, acc_buf):
    for j in range(D_FF // FF_CHUNK):
        cols = slice(j * FF_CHUNK, (j + 1) * FF_CHUNK)
        f = jnp.dot(u_buf[...], wup_ref[:, cols], preferred_element_type=F32)
        f_buf[...] = jnp.square(jnp.maximum(f, 0.0)).astype(BF16)
        part = jnp.dot(f_buf[...], wdown_ref[cols, :], preferred_element_type=F32)
        if j == 0:
            acc_buf[...] = part
        else:
            acc_buf[...] += part


def _mlp_finish(h_ref, mod_ref, gfin_ref, acc_buf, o_ref):
    gate = mod_ref[5:6, :]
    for i in range(h_ref.shape[0] // ROW_BLOCK):
        rows = slice(i * ROW_BLOCK, (i + 1) * ROW_BLOCK)
        hb = h_ref[rows, :] + gate * acc_buf[rows, :]
        ms = jnp.mean(hb * hb, axis=-1, keepdims=True)
        o_ref[rows, :] = hb * lax.rsqrt(ms + EPS) * gfin_ref[...]


def _mlp_kernel(n_tiles, h_ref, hlag_ref, mod_ref, modlag_ref, gmlp_ref, wup_ref, wdown_ref,
                gfin_ref, o_ref, u0, u1, acc0, acc1, f_buf):
    n = pl.program_id(0)
    u_bufs = (u0, u1)
    acc_bufs = (acc0, acc1)

    def step(parity, norm, matmuls, finish):
        if finish:
            _mlp_finish(hlag_ref, modlag_ref, gfin_ref, acc_bufs[parity], o_ref)
        if norm:
            _mlp_norm(h_ref, mod_ref, gmlp_ref, u_bufs[parity])
        if matmuls:
            _mlp_matmuls(wup_ref, wdown_ref, u_bufs[1 - parity], f_buf, acc_bufs[1 - parity])

    @pl.when(n == 0)
    def _():
        step(0, True, False, False)

    @pl.when(n == 1)
    def _():
        step(1, True, True, False)

    steady = (n >= 2) & (n < n_tiles)

    @pl.when(steady & (lax.rem(n, 2) == 0))
    def _():
        step(0, True, True, True)

    @pl.when(steady & (lax.rem(n, 2) == 1))
    def _():
        step(1, True, True, True)

    @pl.when(n == n_tiles)
    def _():
        step(n_tiles % 2, False, True, True)

    @pl.when(n == n_tiles + 1)
    def _():
        step((n_tiles + 1) % 2, False, False, True)


def _mlp(h, mod3, g_mlp, w_up_b, w_down_b, g_final):
    batch, seq, _ = h.shape
    tm = min(MLP_TILE, seq)
    tiles_per_seq = seq // tm
    n_tiles = batch * tiles_per_seq
    assert n_tiles >= 2

    def current(n):
        t = jnp.minimum(n, n_tiles - 1)
        return t // tiles_per_seq, t % tiles_per_seq

    def lagged(n):
        t = jnp.maximum(n - 2, 0)
        return t // tiles_per_seq, t % tiles_per_seq

    return pl.pallas_call(
        functools.partial(_mlp_kernel, n_tiles),
        grid=(n_tiles + 2,),
        in_specs=[
            pl.BlockSpec((None, tm, D_MODEL), lambda n: (*current(n), 0)),
            pl.BlockSpec((None, tm, D_MODEL), lambda n: (*lagged(n), 0)),
            pl.BlockSpec((None, N_MOD, D_MODEL), lambda n: (current(n)[0], 0, 0)),
            pl.BlockSpec((None, N_MOD, D_MODEL), lambda n: (lagged(n)[0], 0, 0)),
            _const_spec((1, D_MODEL)),
            _const_spec((D_MODEL, D_FF)),
            _const_spec((D_FF, D_MODEL)),
            _const_spec((1, D_MODEL)),
        ],
        out_specs=pl.BlockSpec((None, tm, D_MODEL), lambda n: (*lagged(n), 0)),
        out_shape=jax.ShapeDtypeStruct(h.shape, F32),
        scratch_shapes=[
            pltpu.VMEM((tm, D_MODEL), BF16),
            pltpu.VMEM((tm, D_MODEL), BF16),
            pltpu.VMEM((tm, D_MODEL), F32),
            pltpu.VMEM((tm, D_MODEL), F32),
            pltpu.VMEM((tm, FF_CHUNK), BF16),
        ],
        compiler_params=pltpu.CompilerParams(
            dimension_semantics=("arbitrary",), vmem_limit_bytes=VMEM_LIMIT),
        name="relu2_mlp",
    )(h, h, mod3, mod3, g_mlp, w_up_b, w_down_b, g_final)


def _pad_lanes(v):
    return jnp.pad(v.reshape(1, -1), ((0, 0), (0, LANES - v.shape[-1])))


def kernel(x, c, w_ada, b_ada, g_mix, w_in, conv_w, conv_b, dt_bias, a_log, d_skip, g_ssd, w_pool,
           pool_scale, w_out, g_mlp, w_up, w_down, g_final):
    batch = x.shape[0]
    assert w_ada.shape[0] == 1, "the MLP call fuses the final norm, so exactly one layer"
    layer = 0
    mod3 = _modulation(c, w_ada[layer], b_ada[layer]).reshape(batch, N_MOD, D_MODEL)
    w_in_p = jnp.pad(w_in[layer], ((0, 0), (0, IN_PAD - w_in.shape[-1]))).astype(BF16)
    h = _mixer(
        x, mod3, g_mix[layer].reshape(1, -1), w_in_p, conv_w[layer],
        conv_b[layer].reshape(1, -1), _pad_lanes(dt_bias[layer]), _pad_lanes(a_log[layer]),
        jnp.repeat(d_skip[layer], SSD_HEAD_DIM).reshape(1, -1), g_ssd[layer].reshape(1, -1),
        w_pool[layer].astype(BF16), pool_scale[layer].reshape(1, -1),
        w_out[layer].astype(BF16))
    return _mlp(h, mod3, g_mlp[layer].reshape(1, -1), w_up[layer].astype(BF16),
                w_down[layer].astype(BF16), g_final.reshape(1, -1))
```

```python
import functools

import jax
import jax.numpy as jnp
from jax import lax
from jax.experimental import pallas as pl
from jax.experimental.pallas import tpu as pltpu

F32 = jnp.float32
BF16 = jnp.bfloat16

D_MODEL = 1024
POOL_WIDTH = 512
POOL_WINDOWS = (2, 4, 8, 16)
POOL_GROUP = 128
SSD_INNER = 1024
SSD_HEADS = 16
SSD_HEAD_DIM = 64
SSD_GROUPS = 2
SSD_STATE = 128
GROUP_WIDTH = SSD_INNER // SSD_GROUPS
CONV_WIDTH = 4
CHUNK = 128
CONV_CH = SSD_INNER + 2 * SSD_GROUPS * SSD_STATE
MIX_WIDTH = POOL_WIDTH + SSD_INNER
OFF_Z = POOL_WIDTH
OFF_XBC = OFF_Z + SSD_INNER
OFF_DT = OFF_XBC + CONV_CH
LANES = 128
IN_PAD = OFF_DT + LANES
D_FF = 4096
N_MOD = 6
EPS = 1e-5

N_POOL = len(POOL_WINDOWS)
N_XBC = CONV_CH // LANES
N_XS = SSD_INNER // LANES
POOL_HIST = 16
CONV_HIST = 8
MIX_TILE = 512
MLP_TILE = 512
FF_CHUNK = 1024
ROW_BLOCK = 64
TOKEN_PHASES = 4
LOG2_E = 1.4426950408889634
VMEM_LIMIT = 56 * 1024 * 1024


def _const_spec(shape):
    zeros = (0,) * len(shape)
    return pl.BlockSpec(shape, lambda *_: zeros, pipeline_mode=pl.Buffered(1))


def _mod_kernel(c_ref, w_ref, b_ref, o_ref):
    c = c_ref[...]
    c_act = (c * jax.nn.sigmoid(c)).astype(BF16)
    o_ref[...] = jnp.dot(c_act, w_ref[...].astype(BF16), preferred_element_type=F32) + b_ref[...]


def _modulation(c, w_ada, b_ada):
    batch = c.shape[0]
    n_out = w_ada.shape[1]
    return pl.pallas_call(
        _mod_kernel,
        grid=(n_out // D_MODEL,),
        in_specs=[
            pl.BlockSpec((batch, D_MODEL), lambda j: (0, 0)),
            pl.BlockSpec((D_MODEL, D_MODEL), lambda j: (0, j)),
            pl.BlockSpec((1, D_MODEL), lambda j: (0, j)),
        ],
        out_specs=pl.BlockSpec((batch, D_MODEL), lambda j: (0, j)),
        out_shape=jax.ShapeDtypeStruct((batch, n_out), F32),
        name="adaln_mod",
    )(c, w_ada, b_ada.reshape(1, n_out))


def _silu(v):
    half = 0.5 * v
    return half + half * jnp.tanh(half)


def _phase_rows(start, n_rows):
    return pl.ds(start, n_rows // TOKEN_PHASES, stride=TOKEN_PHASES)


def _project_norm(x_ref, mod_ref, gmix_ref, u_buf):
    ts = x_ref.shape[0]
    gain = gmix_ref[...] * (1.0 + mod_ref[1:2, :])
    shift = mod_ref[0:1, :]
    for i in range(ts // ROW_BLOCK):
        rows = slice(i * ROW_BLOCK, (i + 1) * ROW_BLOCK)
        xb = x_ref[rows, :]
        ms = jnp.mean(xb * xb, axis=-1, keepdims=True)
        u_buf[rows, :] = (xb * lax.rsqrt(ms + EPS) * gain + shift).astype(BF16)


PROJECT_PARTS = 4


def _project(part, win_ref, u_buf, slot):
    pool_buf, xbc_buf, z_buf, dt_buf = slot
    ts = u_buf.shape[0]
    u = u_buf[...]
    half_xbc = N_XBC // 2
    z_split = 2 * LANES

    def dot(lo, hi):
        return jnp.dot(u, win_ref[:, lo:hi], preferred_element_type=F32)

    if part in (0, 1):
        first = part * half_xbc
        xbc = dot(OFF_XBC + first * LANES, OFF_XBC + (first + half_xbc) * LANES)
        for j in range(half_xbc):
            xbc_buf[first + j, CONV_HIST:CONV_HIST + ts, :] = xbc[:, j * LANES:(j + 1) * LANES]
    elif part == 2:
        dt_buf[...] = dot(OFF_DT, IN_PAD)
        pooled = dot(0, OFF_Z)
        for g in range(N_POOL):
            pool_buf[g, POOL_HIST:POOL_HIST + ts, :] = pooled[:, g * LANES:(g + 1) * LANES]
        z_buf[:, 0:z_split] = dot(OFF_Z, OFF_Z + z_split)
    else:
        z_buf[:, z_split:SSD_INNER] = dot(OFF_Z + z_split, OFF_XBC)


def _mix_prepare(seq_pos, next_starts_seq, slot, next_slot, dtb_ref, alog_ref, wpool_ref,
                 pscale_ref, p_buf, ac_buf, w_buf, ac_t_buf, dt_t_buf, ymix_buf):
    pool_buf, xbc_buf, _, dt_buf = slot
    next_pool_buf, next_xbc_buf, _, _ = next_slot
    ts = dt_buf.shape[0]
    n_chunks = ts // CHUNK

    dt = jax.nn.softplus(dt_buf[...] + dtb_ref[...])
    da = dt * (-LOG2_E * jnp.exp(alog_ref[...]))
    row_id = lax.broadcasted_iota(jnp.int32, (CHUNK, CHUNK), 0)
    col_id = lax.broadcasted_iota(jnp.int32, (CHUNK, CHUNK), 1)
    tril = (row_id >= col_id).astype(BF16)
    tril3 = jnp.concatenate([tril, tril, tril], axis=1)
    for c in range(n_chunks):
        rows = slice(c * CHUNK, (c + 1) * CHUNK)
        hs = slice(c * SSD_HEADS, (c + 1) * SSD_HEADS)
        da_c = da[rows]
        hi = da_c.astype(BF16)
        rest = da_c - hi.astype(F32)
        mid = rest.astype(BF16)
        low = (rest - mid.astype(F32)).astype(BF16)
        ac = jnp.dot(tril3, jnp.concatenate([hi, mid, low], axis=0),
                     preferred_element_type=F32)
        dt_c = dt[rows]
        ac_buf[rows, :] = ac
        w_buf[rows, :] = dt_c * jnp.exp2(ac[CHUNK - 1:CHUNK, :] - ac)
        ac_t_buf[hs, :] = ac.T[0:SSD_HEADS, :]
        dt_t_buf[hs, :] = dt_c.T[0:SSD_HEADS, :]

    phase_rows = ts // TOKEN_PHASES
    token0 = seq_pos * ts + 1 + TOKEN_PHASES * lax.broadcasted_iota(jnp.int32, (phase_rows, LANES), 0)
    tpos = [(token0 + s).astype(F32) for s in range(TOKEN_PHASES)]
    for g, w in enumerate(POOL_WINDOWS):
        lanes = slice(g * POOL_GROUP, (g + 1) * POOL_GROUP)
        view = {m: pool_buf[g, _phase_rows(POOL_HIST + m, ts), :]
                for m in range(1 - w, TOKEN_PHASES)}
        acc = view[0]
        for k in range(1, w):
            acc = acc + view[-k]
        for s in range(TOKEN_PHASES):
            if s > 0:
                acc = acc + view[s] - view[s - w]
            p_buf[g, _phase_rows(s, ts), :] = acc / jnp.minimum(tpos[s], float(w)) - view[s]
        yp = jnp.dot(p_buf[g].astype(BF16), wpool_ref[g], preferred_element_type=F32)
        ymix_buf[:, lanes] = (yp * pscale_ref[:, lanes]).astype(BF16)
        tail = pool_buf[g, ts:ts + POOL_HIST, :]
        next_pool_buf[g, 0:POOL_HIST, :] = jnp.where(next_starts_seq, jnp.zeros_like(tail), tail)

    for j in range(N_XBC):
        tail = xbc_buf[j, ts:ts + CONV_HIST, :]
        next_xbc_buf[j, 0:CONV_HIST, :] = jnp.where(next_starts_seq, jnp.zeros_like(tail), tail)


def _mix_conv(c, xbc_buf, convw_ref, convb_ref, act_buf):
    r0 = c * CHUNK
    for j in range(N_XBC):
        lanes = slice(j * LANES, (j + 1) * LANES)
        view = [xbc_buf[j, _phase_rows(CONV_HIST + r0 + m, CHUNK), :]
                for m in range(1 - CONV_WIDTH, TOKEN_PHASES)]
        taps = [convw_ref[k:k + 1, lanes] for k in range(CONV_WIDTH)]
        for s in range(TOKEN_PHASES):
            acc = convb_ref[:, lanes] + view[s] * taps[0]
            for k in range(1, CONV_WIDTH):
                acc = acc + view[s + k] * taps[k]
            act_buf[j, _phase_rows(r0 + s, CHUNK), :] = _silu(acc)


def _mix_chunk(c, starts_seq, z_buf, dskip_ref, gssd_ref, act_buf, ac_buf, w_buf, ac_t_buf,
               dt_t_buf, ymix_buf, state_buf, yoff_buf, y_buf):
    rows = slice(c * CHUNK, (c + 1) * CHUNK)
    row_id = lax.broadcasted_iota(jnp.int32, (CHUNK, CHUNK), 0)
    col_id = lax.broadcasted_iota(jnp.int32, (CHUNK, CHUNK), 1)
    causal = row_id >= col_id
    low_half = col_id < SSD_HEAD_DIM
    slabs_per_group = GROUP_WIDTH // LANES
    heads_per_group = SSD_HEADS // SSD_GROUPS
    ac = ac_buf[rows, :]
    w = w_buf[rows, :]
    ac_t = ac_t_buf[c * SSD_HEADS:(c + 1) * SSD_HEADS, :]
    dt_t = dt_t_buf[c * SSD_HEADS:(c + 1) * SSD_HEADS, :]

    def head_columns(table, h):
        return [jnp.broadcast_to(table[:, k:k + 1], (CHUNK, LANES)) for k in (h, h + 1)]

    def carried_state(g):
        prev = state_buf[g]
        if c == 0:
            prev = jnp.where(starts_seq, jnp.zeros_like(prev), prev)
        return prev

    for g in range(SSD_GROUPS):
        b_g = act_buf[N_XS + g, rows, :].astype(BF16)
        c_g = act_buf[N_XS + SSD_GROUPS + g, rows, :].astype(BF16)
        scores = lax.dot_general(c_g, b_g, (((1,), (1,)), ((), ())),
                                 preferred_element_type=F32)
        yoff_buf[g] = jnp.dot(c_g, carried_state(g).astype(BF16), preferred_element_type=F32)
        ssq, a_last, xdec = None, [], []
        for j in range(slabs_per_group):
            slab = g * slabs_per_group + j
            lanes = slice(slab * LANES, (slab + 1) * LANES)
            group_lanes = slice(j * LANES, (j + 1) * LANES)
            h0 = g * heads_per_group + 2 * j
            xs = act_buf[slab, rows, :]
            ac_cols = head_columns(ac, h0)
            w_cols = head_columns(w, h0)
            acx = jnp.where(low_half, ac_cols[0], ac_cols[1])
            a_last.append(acx[CHUNK - 1:CHUNK, :])
            xdec.append((xs * jnp.where(low_half, w_cols[0], w_cols[1])).astype(BF16))
            m_pair = []
            for k in range(2):
                h = h0 + k
                seg = ac_cols[k] - ac_t[h:h + 1, :]
                decay = jnp.exp2(jnp.where(causal, seg, -jnp.inf))
                m_pair.append((scores * decay * dt_t[h:h + 1, :]).astype(BF16))
            lhs = jnp.concatenate(m_pair, axis=1)
            xp = xs.astype(BF16)
            zero = jnp.zeros_like(xp)
            rhs = jnp.concatenate([jnp.where(low_half, xp, zero),
                                   jnp.where(low_half, zero, xp)], axis=0)
            y = (jnp.dot(lhs, rhs, preferred_element_type=F32)
                 + yoff_buf[g, :, group_lanes] * jnp.exp2(acx) + dskip_ref[:, lanes] * xs)
            y = y * _silu(z_buf[rows, lanes])
            y_buf[g, :, group_lanes] = y
            ssq = y * y if ssq is None else ssq + y * y
        new_states = lax.dot_general(b_g, jnp.concatenate(xdec, axis=1),
                                     (((0,), (0,)), ((), ())), preferred_element_type=F32)
        chunk_decay = jnp.exp2(jnp.concatenate(a_last, axis=1))
        state_buf[g] = carried_state(g) * chunk_decay + new_states
        inv = lax.rsqrt(jnp.sum(ssq, axis=-1, keepdims=True) * (1.0 / GROUP_WIDTH) + EPS)
        for j in range(slabs_per_group):
            slab = g * slabs_per_group + j
            lanes = slice(slab * LANES, (slab + 1) * LANES)
            ymix_buf[rows, POOL_WIDTH + slab * LANES:POOL_WIDTH + (slab + 1) * LANES] = (
                y_buf[g, :, j * LANES:(j + 1) * LANES] * inv * gssd_ref[:, lanes]).astype(BF16)


def _mixer_kernel(n_tiles, tiles_per_seq,
                  x_ref, xlag_ref, mod_ref, modlag_ref, gmix_ref, win_ref, convw_ref, convb_ref,
                  dtb_ref, alog_ref, dskip_ref, gssd_ref, wpool_ref, pscale_ref, wout_ref, o_ref,
                  u_buf, pool0, xbc0, z0, dt0, pool1, xbc1, z1, dt1, p_buf, act_buf, ac_buf,
                  w_buf, ac_t_buf, dt_t_buf, ymix_buf, state_buf, yoff_buf, y_buf):
    ts = x_ref.shape[0]
    n = pl.program_id(0)
    slots = ((pool0, xbc0, z0, dt0), (pool1, xbc1, z1, dt1))

    def mix(slot, next_slot, project_slot):
        seq_pos = lax.rem(n - 1, tiles_per_seq)
        starts_seq = seq_pos == 0
        next_starts_seq = seq_pos == tiles_per_seq - 1
        if project_slot is not None:
            _project_norm(x_ref, mod_ref, gmix_ref, u_buf)
        _mix_prepare(seq_pos, next_starts_seq, slot, next_slot, dtb_ref, alog_ref, wpool_ref,
                     pscale_ref, p_buf, ac_buf, w_buf, ac_t_buf, dt_t_buf, ymix_buf)
        n_chunks = ts // CHUNK
        assert n_chunks == PROJECT_PARTS
        _mix_conv(0, slot[1], convw_ref, convb_ref, act_buf)
        for c in range(n_chunks):
            _mix_chunk(c, starts_seq, slot[2], dskip_ref, gssd_ref, act_buf, ac_buf, w_buf,
                       ac_t_buf, dt_t_buf, ymix_buf, state_buf, yoff_buf, y_buf)
            if project_slot is not None:
                _project(c, win_ref, u_buf, project_slot)
            if c + 1 < n_chunks:
                _mix_conv(c + 1, slot[1], convw_ref, convb_ref, act_buf)
        mixed = jnp.dot(ymix_buf[...], wout_ref[...], preferred_element_type=F32)
        o_ref[...] = xlag_ref[...] + modlag_ref[2:3, :] * mixed

    @pl.when(n == 0)
    def _():
        pool0[:, 0:POOL_HIST, :] = jnp.zeros((N_POOL, POOL_HIST, LANES), F32)
        xbc0[:, 0:CONV_HIST, :] = jnp.zeros((N_XBC, CONV_HIST, LANES), F32)
        state_buf[...] = jnp.zeros(state_buf.shape, F32)
        _project_norm(x_ref, mod_ref, gmix_ref, u_buf)
        for part in range(PROJECT_PARTS):
            _project(part, win_ref, u_buf, slots[0])

    steady = (n > 0) & (n < n_tiles)

    @pl.when(steady & (lax.rem(n, 2) == 1))
    def _():
        mix(slots[0], slots[1], slots[1])

    @pl.when(steady & (lax.rem(n, 2) == 0))
    def _():
        mix(slots[1], slots[0], slots[0])

    @pl.when(n == n_tiles)
    def _():
        last = (n_tiles - 1) % 2
        mix(slots[last], slots[1 - last], None)


def _mixer(x, mod3, g_mix, w_in_p, conv_w, conv_b, dtb_p, alog_p, dskip_x, g_ssd, w_pool_b,
           pool_scale, w_out_b):
    batch, seq, _ = x.shape
    ts = min(MIX_TILE, seq)
    tiles_per_seq = seq // ts
    n_tiles = batch * tiles_per_seq

    def current(n):
        t = jnp.minimum(n, n_tiles - 1)
        return t // tiles_per_seq, t % tiles_per_seq

    def lagged(n):
        t = jnp.maximum(n - 1, 0)
        return t // tiles_per_seq, t % tiles_per_seq

    in_specs = [
        pl.BlockSpec((None, ts, D_MODEL), lambda n: (*current(n), 0)),
        pl.BlockSpec((None, ts, D_MODEL), lambda n: (*lagged(n), 0)),
        pl.BlockSpec((None, N_MOD, D_MODEL), lambda n: (current(n)[0], 0, 0)),
        pl.BlockSpec((None, N_MOD, D_MODEL), lambda n: (lagged(n)[0], 0, 0)),
        _const_spec((1, D_MODEL)),
        _const_spec((D_MODEL, IN_PAD)),
        _const_spec((CONV_WIDTH, CONV_CH)),
        _const_spec((1, CONV_CH)),
        _const_spec((1, LANES)),
        _const_spec((1, LANES)),
        _const_spec((1, SSD_INNER)),
        _const_spec((1, SSD_INNER)),
        _const_spec((N_POOL, POOL_GROUP, POOL_GROUP)),
        _const_spec((1, POOL_WIDTH)),
        _const_spec((MIX_WIDTH, D_MODEL)),
    ]
    slot = [
        pltpu.VMEM((N_POOL, POOL_HIST + ts, LANES), F32),
        pltpu.VMEM((N_XBC, CONV_HIST + ts, LANES), F32),
        pltpu.VMEM((ts, SSD_INNER), F32),
        pltpu.VMEM((ts, LANES), F32),
    ]
    scratch = [pltpu.VMEM((ts, D_MODEL), BF16)] + slot + slot + [
        pltpu.VMEM((N_POOL, ts, LANES), F32),
        pltpu.VMEM((N_XBC, ts, LANES), F32),
        pltpu.VMEM((ts, LANES), F32),
        pltpu.VMEM((ts, LANES), F32),
        pltpu.VMEM((ts // CHUNK * SSD_HEADS, CHUNK), F32),
        pltpu.VMEM((ts // CHUNK * SSD_HEADS, CHUNK), F32),
        pltpu.VMEM((ts, MIX_WIDTH), BF16),
        pltpu.VMEM((SSD_GROUPS, SSD_STATE, GROUP_WIDTH), F32),
        pltpu.VMEM((SSD_GROUPS, CHUNK, GROUP_WIDTH), F32),
        pltpu.VMEM((SSD_GROUPS, CHUNK, GROUP_WIDTH), F32),
    ]
    return pl.pallas_call(
        functools.partial(_mixer_kernel, n_tiles, tiles_per_seq),
        grid=(n_tiles + 1,),
        in_specs=in_specs,
        out_specs=pl.BlockSpec((None, ts, D_MODEL), lambda n: (*lagged(n), 0)),
        out_shape=jax.ShapeDtypeStruct(x.shape, F32),
        scratch_shapes=scratch,
        compiler_params=pltpu.CompilerParams(
            dimension_semantics=("arbitrary",), vmem_limit_bytes=VMEM_LIMIT),
        name="token_mixer",
    )(x, x, mod3, mod3, g_mix, w_in_p, conv_w, conv_b, dtb_p, alog_p, dskip_x, g_ssd, w_pool_b,
      pool_scale, w_out_b)


def _mlp_norm(h_ref, mod_ref, gmlp_ref, u_buf):
    gain = gmlp_ref[...] * (1.0 + mod_ref[4:5, :])
    shift = mod_ref[3:4, :]
    for i in range(h_ref.shape[0] // ROW_BLOCK):
        rows = slice(i * ROW_BLOCK, (i + 1) * ROW_BLOCK)
        hb = h_ref[rows, :]
        ms = jnp.mean(hb * hb, axis=-1, keepdims=True)
        u_buf[rows, :] = (hb * lax.rsqrt(ms + EPS) * gain + shift).astype(BF16)


def _mlp_matmuls(wup_ref, wdown_ref, u_buf, f_buf, acc_buf):
    for j in range(D_FF // FF_CHUNK):
        cols = slice(j * FF_CHUNK, (j + 1) * FF_CHUNK)
        f = jnp.dot(u_buf[...], wup_ref[:, cols], preferred_element_type=F32)
        f_buf[...] = jnp.square(jnp.maximum(f, 0.0)).astype(BF16)
        part = jnp.dot(f_buf[...], wdown_ref[cols, :], preferred_element_type=F32)
        if j == 0:
            acc_buf[...] = part
        else:
            acc_buf[...] += part


def _mlp_finish(h_ref, mod_ref, gfin_ref, acc_buf, o_ref):
    gate = mod_ref[5:6, :]
    for i in range(h_ref.shape[0] // ROW_BLOCK):
        rows = slice(i * ROW_BLOCK, (i + 1) * ROW_BLOCK)
        hb = h_ref[rows, :] + gate * acc_buf[rows, :]
        ms = jnp.mean(hb * hb, axis=-1, keepdims=True)
        o_ref[rows, :] = hb * lax.rsqrt(ms + EPS) * gfin_ref[...]


def _mlp_kernel(n_tiles, h_ref, hlag_ref, mod_ref, modlag_ref, gmlp_ref, wup_ref, wdown_ref,
                gfin_ref, o_ref, u0, u1, acc0, acc1, f_buf):
    n = pl.program_id(0)
    u_bufs = (u0, u1)
    acc_bufs = (acc0, acc1)

    def step(parity, norm, matmuls, finish):
        if finish:
            _mlp_finish(hlag_ref, modlag_ref, gfin_ref, acc_bufs[parity], o_ref)
        if norm:
            _mlp_norm(h_ref, mod_ref, gmlp_ref, u_bufs[parity])
        if matmuls:
            _mlp_matmuls(wup_ref, wdown_ref, u_bufs[1 - parity], f_buf, acc_bufs[1 - parity])

    @pl.when(n == 0)
    def _():
        step(0, True, False, False)

    @pl.when(n == 1)
    def _():
        step(1, True, True, False)

    steady = (n >= 2) & (n < n_tiles)

    @pl.when(steady & (lax.rem(n, 2) == 0))
    def _():
        step(0, True, True, True)

    @pl.when(steady & (lax.rem(n, 2) == 1))
    def _():
        step(1, True, True, True)

    @pl.when(n == n_tiles)
    def _():
        step(n_tiles % 2, False, True, True)

    @pl.when(n == n_tiles + 1)
    def _():
        step((n_tiles + 1) % 2, False, False, True)


def _mlp(h, mod3, g_mlp, w_up_b, w_down_b, g_final):
    batch, seq, _ = h.shape
    tm = min(MLP_TILE, seq)
    tiles_per_seq = seq // tm
    n_tiles = batch * tiles_per_seq
    assert n_tiles >= 2

    def current(n):
        t = jnp.minimum(n, n_tiles - 1)
        return t // tiles_per_seq, t % tiles_per_seq

    def lagged(n):
        t = jnp.maximum(n - 2, 0)
        return t // tiles_per_seq, t % tiles_per_seq

    return pl.pallas_call(
        functools.partial(_mlp_kernel, n_tiles),
        grid=(n_tiles + 2,),
        in_specs=[
            pl.BlockSpec((None, tm, D_MODEL), lambda n: (*current(n), 0)),
            pl.BlockSpec((None, tm, D_MODEL), lambda n: (*lagged(n), 0)),
            pl.BlockSpec((None, N_MOD, D_MODEL), lambda n: (current(n)[0], 0, 0)),
            pl.BlockSpec((None, N_MOD, D_MODEL), lambda n: (lagged(n)[0], 0, 0)),
            _const_spec((1, D_MODEL)),
            _const_spec((D_MODEL, D_FF)),
            _const_spec((D_FF, D_MODEL)),
            _const_spec((1, D_MODEL)),
        ],
        out_specs=pl.BlockSpec((None, tm, D_MODEL), lambda n: (*lagged(n), 0)),
        out_shape=jax.ShapeDtypeStruct(h.shape, F32),
        scratch_shapes=[
            pltpu.VMEM((tm, D_MODEL), BF16),
            pltpu.VMEM((tm, D_MODEL), BF16),
            pltpu.VMEM((tm, D_MODEL), F32),
            pltpu.VMEM((tm, D_MODEL), F32),
            pltpu.VMEM((tm, FF_CHUNK), BF16),
        ],
        compiler_params=pltpu.CompilerParams(
            dimension_semantics=("arbitrary",), vmem_limit_bytes=VMEM_LIMIT),
        name="relu2_mlp",
    )(h, h, mod3, mod3, g_mlp, w_up_b, w_down_b, g_final)


def _pad_lanes(v):
    return jnp.pad(v.reshape(1, -1), ((0, 0), (0, LANES - v.shape[-1])))


def kernel(x, c, w_ada, b_ada, g_mix, w_in, conv_w, conv_b, dt_bias, a_log, d_skip, g_ssd, w_pool,
           pool_scale, w_out, g_mlp, w_up, w_down, g_final):
    batch = x.shape[0]
    assert w_ada.shape[0] == 1, "the MLP call fuses the final norm, so exactly one layer"
    layer = 0
    mod3 = _modulation(c, w_ada[layer], b_ada[layer]).reshape(batch, N_MOD, D_MODEL)
    w_in_p = jnp.pad(w_in[layer], ((0, 0), (0, IN_PAD - w_in.shape[-1]))).astype(BF16)
    h = _mixer(
        x, mod3, g_mix[layer].reshape(1, -1), w_in_p, conv_w[layer],
        conv_b[layer].reshape(1, -1), _pad_lanes(dt_bias[layer]), _pad_lanes(a_log[layer]),
        jnp.repeat(d_skip[layer], SSD_HEAD_DIM).reshape(1, -1), g_ssd[layer].reshape(1, -1),
        w_pool[layer].astype(BF16), pool_scale[layer].reshape(1, -1),
        w_out[layer].astype(BF16))
    return _mlp(h, mod3, g_mlp[layer].reshape(1, -1), w_up[layer].astype(BF16),
                w_down[layer].astype(BF16), g_final.reshape(1, -1))
```

```python
import functools

import jax
import jax.numpy as jnp
from jax import lax
from jax.experimental import pallas as pl
from jax.experimental.pallas import tpu as pltpu

F32 = jnp.float32
BF16 = jnp.bfloat16

D_MODEL = 1024
POOL_WIDTH = 512
POOL_WINDOWS = (2, 4, 8, 16)
POOL_GROUP = 128
SSD_INNER = 1024
SSD_HEADS = 16
SSD_HEAD_DIM = 64
SSD_GROUPS = 2
SSD_STATE = 128
GROUP_WIDTH = SSD_INNER // SSD_GROUPS
CONV_WIDTH = 4
CHUNK = 128
CONV_CH = SSD_INNER + 2 * SSD_GROUPS * SSD_STATE
MIX_WIDTH = POOL_WIDTH + SSD_INNER
OFF_Z = POOL_WIDTH
OFF_XBC = OFF_Z + SSD_INNER
OFF_DT = OFF_XBC + CONV_CH
LANES = 128
IN_PAD = OFF_DT + LANES
D_FF = 4096
N_MOD = 6
EPS = 1e-5

N_POOL = len(POOL_WINDOWS)
N_XBC = CONV_CH // LANES
N_XS = SSD_INNER // LANES
POOL_HIST = 16
CONV_HIST = 8
MIX_TILE = 512
MLP_TILE = 512
FF_CHUNK = 4096
ROW_BLOCK = 64
TOKEN_PHASES = 4
LOG2_E = 1.4426950408889634
VMEM_LIMIT = 56 * 1024 * 1024


def _const_spec(shape):
    zeros = (0,) * len(shape)
    return pl.BlockSpec(shape, lambda *_: zeros, pipeline_mode=pl.Buffered(1))


def _mod_kernel(c_ref, w_ref, b_ref, o_ref):
    c = c_ref[...]
    c_act = (c * jax.nn.sigmoid(c)).astype(BF16)
    o_ref[...] = jnp.dot(c_act, w_ref[...].astype(BF16), preferred_element_type=F32) + b_ref[...]


def _modulation(c, w_ada, b_ada):
    batch = c.shape[0]
    n_out = w_ada.shape[1]
    return pl.pallas_call(
        _mod_kernel,
        grid=(n_out // D_MODEL,),
        in_specs=[
            pl.BlockSpec((batch, D_MODEL), lambda j: (0, 0)),
            pl.BlockSpec((D_MODEL, D_MODEL), lambda j: (0, j)),
            pl.BlockSpec((1, D_MODEL), lambda j: (0, j)),
        ],
        out_specs=pl.BlockSpec((batch, D_MODEL), lambda j: (0, j)),
        out_shape=jax.ShapeDtypeStruct((batch, n_out), F32),
        name="adaln_mod",
    )(c, w_ada, b_ada.reshape(1, n_out))


def _silu(v):
    half = 0.5 * v
    return half + half * jnp.tanh(half)


def _phase_rows(start, n_rows):
    return pl.ds(start, n_rows // TOKEN_PHASES, stride=TOKEN_PHASES)


def _project_norm(x_ref, mod_ref, gmix_ref, u_buf):
    ts = x_ref.shape[0]
    gain = gmix_ref[...] * (1.0 + mod_ref[1:2, :])
    shift = mod_ref[0:1, :]
    for i in range(ts // ROW_BLOCK):
        rows = slice(i * ROW_BLOCK, (i + 1) * ROW_BLOCK)
        xb = x_ref[rows, :]
        ms = jnp.mean(xb * xb, axis=-1, keepdims=True)
        u_buf[rows, :] = (xb * lax.rsqrt(ms + EPS) * gain + shift).astype(BF16)


PROJECT_PARTS = 4


def _project(part, win_ref, u_buf, slot):
    pool_buf, xbc_buf, z_buf, dt_buf = slot
    ts = u_buf.shape[0]
    u = u_buf[...]
    half_xbc = N_XBC // 2
    z_split = 2 * LANES

    def dot(lo, hi):
        return jnp.dot(u, win_ref[:, lo:hi], preferred_element_type=F32)

    if part in (0, 1):
        first = part * half_xbc
        xbc = dot(OFF_XBC + first * LANES, OFF_XBC + (first + half_xbc) * LANES)
        for j in range(half_xbc):
            xbc_buf[first + j, CONV_HIST:CONV_HIST + ts, :] = xbc[:, j * LANES:(j + 1) * LANES]
    elif part == 2:
        dt_buf[...] = dot(OFF_DT, IN_PAD)
        pooled = dot(0, OFF_Z)
        for g in range(N_POOL):
            pool_buf[g, POOL_HIST:POOL_HIST + ts, :] = pooled[:, g * LANES:(g + 1) * LANES]
        z_buf[:, 0:z_split] = dot(OFF_Z, OFF_Z + z_split)
    else:
        z_buf[:, z_split:SSD_INNER] = dot(OFF_Z + z_split, OFF_XBC)


def _mix_prepare(seq_pos, next_starts_seq, slot, next_slot, dtb_ref, alog_ref, wpool_ref,
                 pscale_ref, p_buf, ac_buf, w_buf, ac_t_buf, ymix_buf):
    pool_buf, xbc_buf, _, dt_buf = slot
    next_pool_buf, next_xbc_buf, _, _ = next_slot
    ts = dt_buf.shape[0]
    n_chunks = ts // CHUNK

    dt = jax.nn.softplus(dt_buf[...] + dtb_ref[...])
    da = dt * (-LOG2_E * jnp.exp(alog_ref[...]))
    row_id = lax.broadcasted_iota(jnp.int32, (CHUNK, CHUNK), 0)
    col_id = lax.broadcasted_iota(jnp.int32, (CHUNK, CHUNK), 1)
    tril = (row_id >= col_id).astype(BF16)
    tril3 = jnp.concatenate([tril, tril, tril], axis=1)
    for c in range(n_chunks):
        rows = slice(c * CHUNK, (c + 1) * CHUNK)
        hs = slice(c * SSD_HEADS, (c + 1) * SSD_HEADS)
        da_c = da[rows]
        hi = da_c.astype(BF16)
        rest = da_c - hi.astype(F32)
        mid = rest.astype(BF16)
        low = (rest - mid.astype(F32)).astype(BF16)
        ac = jnp.dot(tril3, jnp.concatenate([hi, mid, low], axis=0),
                     preferred_element_type=F32)
        dt_c = dt[rows]
        ac_buf[rows, :] = ac
        w_buf[rows, :] = dt_c * jnp.exp2(ac[CHUNK - 1:CHUNK, :] - ac)
        ac_t_buf[hs, :] = (ac - jnp.log2(dt_c)).T[0:SSD_HEADS, :]

    phase_rows = ts // TOKEN_PHASES
    token0 = seq_pos * ts + 1 + TOKEN_PHASES * lax.broadcasted_iota(jnp.int32, (phase_rows, LANES), 0)
    tpos = [(token0 + s).astype(F32) for s in range(TOKEN_PHASES)]
    for g, w in enumerate(POOL_WINDOWS):
        lanes = slice(g * POOL_GROUP, (g + 1) * POOL_GROUP)
        view = {m: pool_buf[g, _phase_rows(POOL_HIST + m, ts), :]
                for m in range(1 - w, TOKEN_PHASES)}
        acc = view[0]
        for k in range(1, w):
            acc = acc + view[-k]
        for s in range(TOKEN_PHASES):
            if s > 0:
                acc = acc + view[s] - view[s - w]
            p_buf[g, _phase_rows(s, ts), :] = acc / jnp.minimum(tpos[s], float(w)) - view[s]
        yp = jnp.dot(p_buf[g].astype(BF16), wpool_ref[g], preferred_element_type=F32)
        ymix_buf[:, lanes] = (yp * pscale_ref[:, lanes]).astype(BF16)
        tail = pool_buf[g, ts:ts + POOL_HIST, :]
        next_pool_buf[g, 0:POOL_HIST, :] = jnp.where(next_starts_seq, jnp.zeros_like(tail), tail)

    for j in range(N_XBC):
        tail = xbc_buf[j, ts:ts + CONV_HIST, :]
        next_xbc_buf[j, 0:CONV_HIST, :] = jnp.where(next_starts_seq, jnp.zeros_like(tail), tail)


def _mix_conv(c, xbc_buf, convw_ref, convb_ref, act_buf):
    r0 = c * CHUNK
    for j in range(N_XBC):
        lanes = slice(j * LANES, (j + 1) * LANES)
        view = [xbc_buf[j, _phase_rows(CONV_HIST + r0 + m, CHUNK), :]
                for m in range(1 - CONV_WIDTH, TOKEN_PHASES)]
        taps = [convw_ref[k:k + 1, lanes] for k in range(CONV_WIDTH)]
        for s in range(TOKEN_PHASES):
            acc = convb_ref[:, lanes] + view[s] * taps[0]
            for k in range(1, CONV_WIDTH):
                acc = acc + view[s + k] * taps[k]
            act_buf[j, _phase_rows(r0 + s, CHUNK), :] = _silu(acc)


def _mix_chunk(c, starts_seq, z_buf, dskip_ref, gssd_ref, act_buf, ac_buf, w_buf, ac_t_buf,
               ymix_buf, state_buf, yoff_buf, y_buf):
    rows = slice(c * CHUNK, (c + 1) * CHUNK)
    row_id = lax.broadcasted_iota(jnp.int32, (CHUNK, CHUNK), 0)
    col_id = lax.broadcasted_iota(jnp.int32, (CHUNK, CHUNK), 1)
    causal = row_id >= col_id
    low_half = col_id < SSD_HEAD_DIM
    slabs_per_group = GROUP_WIDTH // LANES
    heads_per_group = SSD_HEADS // SSD_GROUPS
    ac = ac_buf[rows, :]
    w = w_buf[rows, :]
    ac_t = ac_t_buf[c * SSD_HEADS:(c + 1) * SSD_HEADS, :]

    def head_columns(table, h):
        return [jnp.broadcast_to(table[:, k:k + 1], (CHUNK, LANES)) for k in (h, h + 1)]

    def carried_state(g):
        prev = state_buf[g]
        if c == 0:
            prev = jnp.where(starts_seq, jnp.zeros_like(prev), prev)
        return prev

    for g in range(SSD_GROUPS):
        b_g = act_buf[N_XS + g, rows, :].astype(BF16)
        c_g = act_buf[N_XS + SSD_GROUPS + g, rows, :].astype(BF16)
        scores_bf = lax.dot_general(c_g, b_g, (((1,), (1,)), ((), ())),
                                    preferred_element_type=F32).astype(BF16)
        yoff_buf[g] = jnp.dot(c_g, carried_state(g).astype(BF16), preferred_element_type=F32)
        ssq, a_last, xdec = None, [], []
        for j in range(slabs_per_group):
            slab = g * slabs_per_group + j
            lanes = slice(slab * LANES, (slab + 1) * LANES)
            group_lanes = slice(j * LANES, (j + 1) * LANES)
            h0 = g * heads_per_group + 2 * j
            xs = act_buf[slab, rows, :]
            ac_cols = head_columns(ac, h0)
            w_cols = head_columns(w, h0)
            acx = jnp.where(low_half, ac_cols[0], ac_cols[1])
            a_last.append(acx[CHUNK - 1:CHUNK, :])
            xdec.append((xs * jnp.where(low_half, w_cols[0], w_cols[1])).astype(BF16))
            m_pair = []
            for k in range(2):
                h = h0 + k
                seg = ac_cols[k] - ac_t[h:h + 1, :]
                decay = jnp.exp2(jnp.where(causal, seg, -jnp.inf))
                m_pair.append(scores_bf * decay.astype(BF16))
            lhs = jnp.concatenate(m_pair, axis=1)
            xp = xs.astype(BF16)
            zero = jnp.zeros_like(xp)
            rhs = jnp.concatenate([jnp.where(low_half, xp, zero),
                                   jnp.where(low_half, zero, xp)], axis=0)
            y = (jnp.dot(lhs, rhs, preferred_element_type=F32)
                 + yoff_buf[g, :, group_lanes] * jnp.exp2(acx) + dskip_ref[:, lanes] * xs)
            y = y * _silu(z_buf[rows, lanes])
            y_buf[g, :, group_lanes] = y
            ssq = y * y if ssq is None else ssq + y * y
        new_states = lax.dot_general(b_g, jnp.concatenate(xdec, axis=1),
                                     (((0,), (0,)), ((), ())), preferred_element_type=F32)
        chunk_decay = jnp.exp2(jnp.concatenate(a_last, axis=1))
        state_buf[g] = carried_state(g) * chunk_decay + new_states
        inv = lax.rsqrt(jnp.sum(ssq, axis=-1, keepdims=True) * (1.0 / GROUP_WIDTH) + EPS)
        for j in range(slabs_per_group):
            slab = g * slabs_per_group + j
            lanes = slice(slab * LANES, (slab + 1) * LANES)
            ymix_buf[rows, POOL_WIDTH + slab * LANES:POOL_WIDTH + (slab + 1) * LANES] = (
                y_buf[g, :, j * LANES:(j + 1) * LANES] * inv * gssd_ref[:, lanes]).astype(BF16)


def _mixer_kernel(n_tiles, tiles_per_seq,
                  x_ref, xlag_ref, mod_ref, modlag_ref, gmix_ref, win_ref, convw_ref, convb_ref,
                  dtb_ref, alog_ref, dskip_ref, gssd_ref, wpool_ref, pscale_ref, wout_ref, o_ref,
                  u_buf, pool0, xbc0, z0, dt0, pool1, xbc1, z1, dt1, p_buf, act_buf, ac_buf,
                  w_buf, ac_t_buf, ymix_buf, state_buf, yoff_buf, y_buf):
    ts = x_ref.shape[0]
    n = pl.program_id(0)
    slots = ((pool0, xbc0, z0, dt0), (pool1, xbc1, z1, dt1))

    def mix(slot, next_slot, project_slot):
        seq_pos = lax.rem(n - 1, tiles_per_seq)
        starts_seq = seq_pos == 0
        next_starts_seq = seq_pos == tiles_per_seq - 1
        if project_slot is not None:
            _project_norm(x_ref, mod_ref, gmix_ref, u_buf)
        _mix_prepare(seq_pos, next_starts_seq, slot, next_slot, dtb_ref, alog_ref, wpool_ref,
                     pscale_ref, p_buf, ac_buf, w_buf, ac_t_buf, ymix_buf)
        n_chunks = ts // CHUNK
        for c in range(n_chunks):
            _mix_conv(c, slot[1], convw_ref, convb_ref, act_buf)
        if project_slot is not None:
            for part in range(PROJECT_PARTS):
                _project(part, win_ref, u_buf, project_slot)
        for c in range(n_chunks):
            _mix_chunk(c, starts_seq, slot[2], dskip_ref, gssd_ref, act_buf, ac_buf, w_buf,
                       ac_t_buf, ymix_buf, state_buf, yoff_buf, y_buf)
        mixed = jnp.dot(ymix_buf[...], wout_ref[...], preferred_element_type=F32)
        o_ref[...] = xlag_ref[...] + modlag_ref[2:3, :] * mixed

    @pl.when(n == 0)
    def _():
        pool0[:, 0:POOL_HIST, :] = jnp.zeros((N_POOL, POOL_HIST, LANES), F32)
        xbc0[:, 0:CONV_HIST, :] = jnp.zeros((N_XBC, CONV_HIST, LANES), F32)
        state_buf[...] = jnp.zeros(state_buf.shape, F32)
        _project_norm(x_ref, mod_ref, gmix_ref, u_buf)
        for part in range(PROJECT_PARTS):
            _project(part, win_ref, u_buf, slots[0])

    steady = (n > 0) & (n < n_tiles)

    @pl.when(steady & (lax.rem(n, 2) == 1))
    def _():
        mix(slots[0], slots[1], slots[1])

    @pl.when(steady & (lax.rem(n, 2) == 0))
    def _():
        mix(slots[1], slots[0], slots[0])

    @pl.when(n == n_tiles)
    def _():
        last = (n_tiles - 1) % 2
        mix(slots[last], slots[1 - last], None)


def _mixer(x, mod3, g_mix, w_in_p, conv_w, conv_b, dtb_p, alog_p, dskip_x, g_ssd, w_pool_b,
           pool_scale, w_out_b):
    batch, seq, _ = x.shape
    ts = min(MIX_TILE, seq)
    tiles_per_seq = seq // ts
    n_tiles = batch * tiles_per_seq

    def current(n):
        t = jnp.minimum(n, n_tiles - 1)
        return t // tiles_per_seq, t % tiles_per_seq

    def lagged(n):
        t = jnp.maximum(n - 1, 0)
        return t // tiles_per_seq, t % tiles_per_seq

    in_specs = [
        pl.BlockSpec((None, ts, D_MODEL), lambda n: (*current(n), 0)),
        pl.BlockSpec((None, ts, D_MODEL), lambda n: (*lagged(n), 0)),
        pl.BlockSpec((None, N_MOD, D_MODEL), lambda n: (current(n)[0], 0, 0)),
        pl.BlockSpec((None, N_MOD, D_MODEL), lambda n: (lagged(n)[0], 0, 0)),
        _const_spec((1, D_MODEL)),
        _const_spec((D_MODEL, IN_PAD)),
        _const_spec((CONV_WIDTH, CONV_CH)),
        _const_spec((1, CONV_CH)),
        _const_spec((1, LANES)),
        _const_spec((1, LANES)),
        _const_spec((1, SSD_INNER)),
        _const_spec((1, SSD_INNER)),
        _const_spec((N_POOL, POOL_GROUP, POOL_GROUP)),
        _const_spec((1, POOL_WIDTH)),
        _const_spec((MIX_WIDTH, D_MODEL)),
    ]
    slot = [
        pltpu.VMEM((N_POOL, POOL_HIST + ts, LANES), F32),
        pltpu.VMEM((N_XBC, CONV_HIST + ts, LANES), F32),
        pltpu.VMEM((ts, SSD_INNER), F32),
        pltpu.VMEM((ts, LANES), F32),
    ]
    scratch = [pltpu.VMEM((ts, D_MODEL), BF16)] + slot + slot + [
        pltpu.VMEM((N_POOL, ts, LANES), F32),
        pltpu.VMEM((N_XBC, ts, LANES), F32),
        pltpu.VMEM((ts, LANES), F32),
        pltpu.VMEM((ts, LANES), F32),
        pltpu.VMEM((ts // CHUNK * SSD_HEADS, CHUNK), F32),
        pltpu.VMEM((ts, MIX_WIDTH), BF16),
        pltpu.VMEM((SSD_GROUPS, SSD_STATE, GROUP_WIDTH), F32),
        pltpu.VMEM((SSD_GROUPS, CHUNK, GROUP_WIDTH), F32),
        pltpu.VMEM((SSD_GROUPS, CHUNK, GROUP_WIDTH), F32),
    ]
    return pl.pallas_call(
        functools.partial(_mixer_kernel, n_tiles, tiles_per_seq),
        grid=(n_tiles + 1,),
        in_specs=in_specs,
        out_specs=pl.BlockSpec((None, ts, D_MODEL), lambda n: (*lagged(n), 0)),
        out_shape=jax.ShapeDtypeStruct(x.shape, F32),
        scratch_shapes=scratch,
        compiler_params=pltpu.CompilerParams(
            dimension_semantics=("arbitrary",), vmem_limit_bytes=VMEM_LIMIT),
        name="token_mixer",
    )(x, x, mod3, mod3, g_mix, w_in_p, conv_w, conv_b, dtb_p, alog_p, dskip_x, g_ssd, w_pool_b,
      pool_scale, w_out_b)


def _mlp_norm(h_ref, mod_ref, gmlp_ref, u_buf):
    gain = gmlp_ref[...] * (1.0 + mod_ref[4:5, :])
    shift = mod_ref[3:4, :]
    for i in range(h_ref.shape[0] // ROW_BLOCK):
        rows = slice(i * ROW_BLOCK, (i + 1) * ROW_BLOCK)
        hb = h_ref[rows, :]
        ms = jnp.mean(hb * hb, axis=-1, keepdims=True)
        u_buf[rows, :] = (hb * lax.rsqrt(ms + EPS) * gain + shift).astype(BF16)


def _mlp_matmuls(wup_ref, wdown_ref, u_buf, f_buf, acc_buf):
    for j in range(D_FF // FF_CHUNK):
        cols = slice(j * FF_CHUNK, (j + 1) * FF_CHUNK)
        f = jnp.dot(u_buf[...], wup_ref[:, cols], preferred_element_type=F32)
        f_buf[...] = jnp.square(jnp.maximum(f, 0.0)).astype(BF16)
        part = jnp.dot(f_buf[...], wdown_ref[cols, :], preferred_element_type=F32)
        if j == 0:
            acc_buf[...] = part
        else:
            acc_buf[...] += part


def _mlp_finish(h_ref, mod_ref, gfin_ref, acc_buf, o_ref):
    gate = mod_ref[5:6, :]
    for i in range(h_ref.shape[0] // ROW_BLOCK):
        rows = slice(i * ROW_BLOCK, (i + 1) * ROW_BLOCK)
        hb = h_ref[rows, :] + gate * acc_buf[rows, :]
        ms = jnp.mean(hb * hb, axis=-1, keepdims=True)
        o_ref[rows, :] = hb * lax.rsqrt(ms + EPS) * gfin_ref[...]


def _mlp_kernel(n_tiles, h_ref, hlag_ref, mod_ref, modlag_ref, gmlp_ref, wup_ref, wdown_ref,
                gfin_ref, o_ref, u0, u1, acc0, acc1, f_buf):
    n = pl.program_id(0)
    u_bufs = (u0, u1)
    acc_bufs = (acc0, acc1)

    def step(parity, norm, matmuls, finish):
        if finish:
            _mlp_finish(hlag_ref, modlag_ref, gfin_ref, acc_bufs[parity], o_ref)
        if norm:
            _mlp_norm(h_ref, mod_ref, gmlp_ref, u_bufs[parity])
        if matmuls:
            _mlp_matmuls(wup_ref, wdown_ref, u_bufs[1 - parity], f_buf, acc_bufs[1 - parity])

    @pl.when(n == 0)
    def _():
        step(0, True, False, False)

    @pl.when(n == 1)
    def _():
        step(1, True, True, False)

    steady = (n >= 2) & (n < n_tiles)

    @pl.when(steady & (lax.rem(n, 2) == 0))
    def _():
        step(0, True, True, True)

    @pl.when(steady & (lax.rem(n, 2) == 1))
    def _():
        step(1, True, True, True)

    @pl.when(n == n_tiles)
    def _():
        step(n_tiles % 2, False, True, True)

    @pl.when(n == n_tiles + 1)
    def _():
        step((n_tiles + 1) % 2, False, False, True)


def _mlp(h, mod3, g_mlp, w_up_b, w_down_b, g_final):
    batch, seq, _ = h.shape
    tm = min(MLP_TILE, seq)
    tiles_per_seq = seq // tm
    n_tiles = batch * tiles_per_seq
    assert n_tiles >= 2

    def current(n):
        t = jnp.minimum(n, n_tiles - 1)
        return t // tiles_per_seq, t % tiles_per_seq

    def lagged(n):
        t = jnp.maximum(n - 2, 0)
        return t // tiles_per_seq, t % tiles_per_seq

    return pl.pallas_call(
        functools.partial(_mlp_kernel, n_tiles),
        grid=(n_tiles + 2,),
        in_specs=[
            pl.BlockSpec((None, tm, D_MODEL), lambda n: (*current(n), 0)),
            pl.BlockSpec((None, tm, D_MODEL), lambda n: (*lagged(n), 0)),
            pl.BlockSpec((None, N_MOD, D_MODEL), lambda n: (current(n)[0], 0, 0)),
            pl.BlockSpec((None, N_MOD, D_MODEL), lambda n: (lagged(n)[0], 0, 0)),
            _const_spec((1, D_MODEL)),
            _const_spec((D_MODEL, D_FF)),
            _const_spec((D_FF, D_MODEL)),
            _const_spec((1, D_MODEL)),
        ],
        out_specs=pl.BlockSpec((None, tm, D_MODEL), lambda n: (*lagged(n), 0)),
        out_shape=jax.ShapeDtypeStruct(h.shape, F32),
        scratch_shapes=[
            pltpu.VMEM((tm, D_MODEL), BF16),
            pltpu.VMEM((tm, D_MODEL), BF16),
            pltpu.VMEM((tm, D_MODEL), F32),
            pltpu.VMEM((tm, D_MODEL), F32),
            pltpu.VMEM((tm, FF_CHUNK), BF16),
        ],
        compiler_params=pltpu.CompilerParams(
            dimension_semantics=("arbitrary",), vmem_limit_bytes=VMEM_LIMIT),
        name="relu2_mlp",
    )(h, h, mod3, mod3, g_mlp, w_up_b, w_down_b, g_final)


def _pad_lanes(v):
    return jnp.pad(v.reshape(1, -1), ((0, 0), (0, LANES - v.shape[-1])))


def kernel(x, c, w_ada, b_ada, g_mix, w_in, conv_w, conv_b, dt_bias, a_log, d_skip, g_ssd, w_pool,
           pool_scale, w_out, g_mlp, w_up, w_down, g_final):
    batch = x.shape[0]
    assert w_ada.shape[0] == 1, "the MLP call fuses the final norm, so exactly one layer"
    layer = 0
    mod3 = _modulation(c, w_ada[layer], b_ada[layer]).reshape(batch, N_MOD, D_MODEL)
    w_in_p = jnp.pad(w_in[layer], ((0, 0), (0, IN_PAD - w_in.shape[-1]))).astype(BF16)
    h = _mixer(
        x, mod3, g_mix[layer].reshape(1, -1), w_in_p, conv_w[layer],
        conv_b[layer].reshape(1, -1), _pad_lanes(dt_bias[layer]), _pad_lanes(a_log[layer]),
        jnp.repeat(d_skip[layer], SSD_HEAD_DIM).reshape(1, -1), g_ssd[layer].reshape(1, -1),
        w_pool[layer].astype(BF16), pool_scale[layer].reshape(1, -1),
        w_out[layer].astype(BF16))
    return _mlp(h, mod3, g_mlp[layer].reshape(1, -1), w_up[layer].astype(BF16),
                w_down[layer].astype(BF16), g_final.reshape(1, -1))
```

```python
import functools

import jax
import jax.numpy as jnp
from jax import lax
from jax.experimental import pallas as pl
from jax.experimental.pallas import tpu as pltpu

F32 = jnp.float32
BF16 = jnp.bfloat16

D_MODEL = 1024
POOL_WIDTH = 512
POOL_WINDOWS = (2, 4, 8, 16)
POOL_GROUP = 128
SSD_INNER = 1024
SSD_HEADS = 16
SSD_HEAD_DIM = 64
SSD_GROUPS = 2
SSD_STATE = 128
GROUP_WIDTH = SSD_INNER // SSD_GROUPS
CONV_WIDTH = 4
CHUNK = 128
CONV_CH = SSD_INNER + 2 * SSD_GROUPS * SSD_STATE
MIX_WIDTH = POOL_WIDTH + SSD_INNER
OFF_Z = POOL_WIDTH
OFF_XBC = OFF_Z + SSD_INNER
OFF_DT = OFF_XBC + CONV_CH
LANES = 128
IN_PAD = OFF_DT + LANES
D_FF = 4096
N_MOD = 6
EPS = 1e-5

N_POOL = len(POOL_WINDOWS)
N_XBC = CONV_CH // LANES
N_XS = SSD_INNER // LANES
POOL_HIST = 16
CONV_HIST = 8
MIX_TILE = 512
MLP_TILE = 512
FF_CHUNK = 4096
ROW_BLOCK = 64
TOKEN_PHASES = 4
LOG2_E = 1.4426950408889634
VMEM_LIMIT = 56 * 1024 * 1024


def _const_spec(shape):
    zeros = (0,) * len(shape)
    return pl.BlockSpec(shape, lambda *_: zeros, pipeline_mode=pl.Buffered(1))


def _mod_kernel(c_ref, w_ref, b_ref, o_ref):
    c = c_ref[...]
    c_act = (c * jax.nn.sigmoid(c)).astype(BF16)
    o_ref[...] = jnp.dot(c_act, w_ref[...].astype(BF16), preferred_element_type=F32) + b_ref[...]


def _modulation(c, w_ada, b_ada):
    batch = c.shape[0]
    n_out = w_ada.shape[1]
    return pl.pallas_call(
        _mod_kernel,
        grid=(n_out // D_MODEL,),
        in_specs=[
            pl.BlockSpec((batch, D_MODEL), lambda j: (0, 0)),
            pl.BlockSpec((D_MODEL, D_MODEL), lambda j: (0, j)),
            pl.BlockSpec((1, D_MODEL), lambda j: (0, j)),
        ],
        out_specs=pl.BlockSpec((batch, D_MODEL), lambda j: (0, j)),
        out_shape=jax.ShapeDtypeStruct((batch, n_out), F32),
        name="adaln_mod",
    )(c, w_ada, b_ada.reshape(1, n_out))


def _silu(v):
    half = 0.5 * v
    return half + half * jnp.tanh(half)


def _phase_rows(start, n_rows):
    return pl.ds(start, n_rows // TOKEN_PHASES, stride=TOKEN_PHASES)


def _project_norm(x_ref, mod_ref, gmix_ref, u_buf):
    ts = x_ref.shape[0]
    gain = gmix_ref[...] * (1.0 + mod_ref[1:2, :])
    shift = mod_ref[0:1, :]
    for i in range(ts // ROW_BLOCK):
        rows = slice(i * ROW_BLOCK, (i + 1) * ROW_BLOCK)
        xb = x_ref[rows, :]
        ms = jnp.mean(xb * xb, axis=-1, keepdims=True)
        u_buf[rows, :] = (xb * lax.rsqrt(ms + EPS) * gain + shift).astype(BF16)


PROJECT_PARTS = 4


def _project(part, win_ref, u_buf, slot):
    pool_buf, xbc_buf, z_buf, dt_buf = slot
    ts = u_buf.shape[0]
    u = u_buf[...]
    half_xbc = N_XBC // 2
    z_split = 2 * LANES

    def dot(lo, hi):
        return jnp.dot(u, win_ref[:, lo:hi], preferred_element_type=F32)

    if part in (0, 1):
        first = part * half_xbc
        xbc = dot(OFF_XBC + first * LANES, OFF_XBC + (first + half_xbc) * LANES)
        for j in range(half_xbc):
            xbc_buf[first + j, CONV_HIST:CONV_HIST + ts, :] = xbc[:, j * LANES:(j + 1) * LANES]
    elif part == 2:
        dt_buf[...] = dot(OFF_DT, IN_PAD)
        pooled = dot(0, OFF_Z)
        for g in range(N_POOL):
            pool_buf[g, POOL_HIST:POOL_HIST + ts, :] = pooled[:, g * LANES:(g + 1) * LANES]
        z_buf[:, 0:z_split] = dot(OFF_Z, OFF_Z + z_split)
    else:
        z_buf[:, z_split:SSD_INNER] = dot(OFF_Z + z_split, OFF_XBC)


def _mix_prepare(seq_pos, next_starts_seq, slot, next_slot, dtb_ref, alog_ref, wpool_ref,
                 pscale_ref, p_buf, ac_buf, w_buf, ac_t_buf, ymix_buf):
    pool_buf, xbc_buf, _, dt_buf = slot
    next_pool_buf, next_xbc_buf, _, _ = next_slot
    ts = dt_buf.shape[0]
    n_chunks = ts // CHUNK

    dt = jax.nn.softplus(dt_buf[...] + dtb_ref[...])
    da = dt * (-LOG2_E * jnp.exp(alog_ref[...]))
    row_id = lax.broadcasted_iota(jnp.int32, (CHUNK, CHUNK), 0)
    col_id = lax.broadcasted_iota(jnp.int32, (CHUNK, CHUNK), 1)
    tril = (row_id >= col_id).astype(BF16)
    tril3 = jnp.concatenate([tril, tril, tril], axis=1)
    for c in range(n_chunks):
        rows = slice(c * CHUNK, (c + 1) * CHUNK)
        hs = slice(c * SSD_HEADS, (c + 1) * SSD_HEADS)
        da_c = da[rows]
        hi = da_c.astype(BF16)
        rest = da_c - hi.astype(F32)
        mid = rest.astype(BF16)
        low = (rest - mid.astype(F32)).astype(BF16)
        ac = jnp.dot(tril3, jnp.concatenate([hi, mid, low], axis=0),
                     preferred_element_type=F32)
        dt_c = dt[rows]
        ac_buf[rows, :] = ac
        w_buf[rows, :] = dt_c * jnp.exp2(ac[CHUNK - 1:CHUNK, :] - ac)
        ac_t_buf[hs, :] = (ac - jnp.log2(dt_c)).T[0:SSD_HEADS, :]

    phase_rows = ts // TOKEN_PHASES
    token0 = seq_pos * ts + 1 + TOKEN_PHASES * lax.broadcasted_iota(jnp.int32, (phase_rows, LANES), 0)
    tpos = [(token0 + s).astype(F32) for s in range(TOKEN_PHASES)]
    for g, w in enumerate(POOL_WINDOWS):
        lanes = slice(g * POOL_GROUP, (g + 1) * POOL_GROUP)
        view = {m: pool_buf[g, _phase_rows(POOL_HIST + m, ts), :]
                for m in range(1 - w, TOKEN_PHASES)}
        acc = view[0]
        for k in range(1, w):
            acc = acc + view[-k]
        for s in range(TOKEN_PHASES):
            if s > 0:
                acc = acc + view[s] - view[s - w]
            p_buf[g, _phase_rows(s, ts), :] = acc / jnp.minimum(tpos[s], float(w)) - view[s]
        yp = jnp.dot(p_buf[g].astype(BF16), wpool_ref[g], preferred_element_type=F32)
        ymix_buf[:, lanes] = (yp * pscale_ref[:, lanes]).astype(BF16)
        tail = pool_buf[g, ts:ts + POOL_HIST, :]
        next_pool_buf[g, 0:POOL_HIST, :] = jnp.where(next_starts_seq, jnp.zeros_like(tail), tail)

    for j in range(N_XBC):
        tail = xbc_buf[j, ts:ts + CONV_HIST, :]
        next_xbc_buf[j, 0:CONV_HIST, :] = jnp.where(next_starts_seq, jnp.zeros_like(tail), tail)


def _mix_conv(c, xbc_buf, convw_ref, convb_ref, act_buf):
    r0 = c * CHUNK
    for j in range(N_XBC):
        lanes = slice(j * LANES, (j + 1) * LANES)
        view = [xbc_buf[j, _phase_rows(CONV_HIST + r0 + m, CHUNK), :]
                for m in range(1 - CONV_WIDTH, TOKEN_PHASES)]
        taps = [convw_ref[k:k + 1, lanes] for k in range(CONV_WIDTH)]
        for s in range(TOKEN_PHASES):
            acc = convb_ref[:, lanes] + view[s] * taps[0]
            for k in range(1, CONV_WIDTH):
                acc = acc + view[s + k] * taps[k]
            act_buf[j, _phase_rows(r0 + s, CHUNK), :] = _silu(acc)


def _mix_chunk(c, starts_seq, z_buf, dskip_ref, gssd_ref, act_buf, ac_buf, w_buf, ac_t_buf,
               ymix_buf, state_buf, yoff_buf, y_buf):
    rows = slice(c * CHUNK, (c + 1) * CHUNK)
    row_id = lax.broadcasted_iota(jnp.int32, (CHUNK, CHUNK), 0)
    col_id = lax.broadcasted_iota(jnp.int32, (CHUNK, CHUNK), 1)
    causal = row_id >= col_id
    low_half = col_id < SSD_HEAD_DIM
    slabs_per_group = GROUP_WIDTH // LANES
    heads_per_group = SSD_HEADS // SSD_GROUPS
    ac = ac_buf[rows, :]
    w = w_buf[rows, :]
    ac_t = ac_t_buf[c * SSD_HEADS:(c + 1) * SSD_HEADS, :]

    def head_columns(table, h):
        return [jnp.broadcast_to(table[:, k:k + 1], (CHUNK, LANES)) for k in (h, h + 1)]

    def carried_state(g):
        prev = state_buf[g]
        if c == 0:
            prev = jnp.where(starts_seq, jnp.zeros_like(prev), prev)
        return prev

    for g in range(SSD_GROUPS):
        b_g = act_buf[N_XS + g, rows, :].astype(BF16)
        c_g = act_buf[N_XS + SSD_GROUPS + g, rows, :].astype(BF16)
        scores_bf = lax.dot_general(c_g, b_g, (((1,), (1,)), ((), ())),
                                    preferred_element_type=F32).astype(BF16)
        yoff_buf[g] = jnp.dot(c_g, carried_state(g).astype(BF16), preferred_element_type=F32)
        ssq, a_last, xdec = None, [], []
        for j in range(slabs_per_group):
            slab = g * slabs_per_group + j
            lanes = slice(slab * LANES, (slab + 1) * LANES)
            group_lanes = slice(j * LANES, (j + 1) * LANES)
            h0 = g * heads_per_group + 2 * j
            xs = act_buf[slab, rows, :]
            ac_cols = head_columns(ac, h0)
            w_cols = head_columns(w, h0)
            acx = jnp.where(low_half, ac_cols[0], ac_cols[1])
            a_last.append(acx[CHUNK - 1:CHUNK, :])
            xdec.append((xs * jnp.where(low_half, w_cols[0], w_cols[1])).astype(BF16))
            m_pair = []
            for k in range(2):
                h = h0 + k
                seg = ac_cols[k] - ac_t[h:h + 1, :]
                decay = jnp.exp2(jnp.where(causal, seg, -jnp.inf))
                m_pair.append(scores_bf * decay.astype(BF16))
            lhs = jnp.concatenate(m_pair, axis=1)
            xp = xs.astype(BF16)
            zero = jnp.zeros_like(xp)
            rhs = jnp.concatenate([jnp.where(low_half, xp, zero),
                                   jnp.where(low_half, zero, xp)], axis=0)
            y = (jnp.dot(lhs, rhs, preferred_element_type=F32)
                 + yoff_buf[g, :, group_lanes] * jnp.exp2(acx) + dskip_ref[:, lanes] * xs)
            y = y * _silu(z_buf[rows, lanes])
            y_buf[g, :, group_lanes] = y
            ssq = y * y if ssq is None else ssq + y * y
        new_states = lax.dot_general(b_g, jnp.concatenate(xdec, axis=1),
                                     (((0,), (0,)), ((), ())), preferred_element_type=F32)
        chunk_decay = jnp.exp2(jnp.concatenate(a_last, axis=1))
        state_buf[g] = carried_state(g) * chunk_decay + new_states
        inv = lax.rsqrt(jnp.sum(ssq, axis=-1, keepdims=True) * (1.0 / GROUP_WIDTH) + EPS)
        for j in range(slabs_per_group):
            slab = g * slabs_per_group + j
            lanes = slice(slab * LANES, (slab + 1) * LANES)
            ymix_buf[rows, POOL_WIDTH + slab * LANES:POOL_WIDTH + (slab + 1) * LANES] = (
                y_buf[g, :, j * LANES:(j + 1) * LANES] * inv * gssd_ref[:, lanes]).astype(BF16)


def _mixer_kernel(n_tiles, tiles_per_seq,
                  x_ref, xlag_ref, mod_ref, modlag_ref, gmix_ref, win_ref, convw_ref, convb_ref,
                  dtb_ref, alog_ref, dskip_ref, gssd_ref, wpool_ref, pscale_ref, wout_ref, o_ref,
                  u_buf, pool0, xbc0, z0, dt0, pool1, xbc1, z1, dt1, p_buf, act_buf, ac_buf,
                  w_buf, ac_t_buf, ymix0, ymix1, state_buf, yoff_buf, y_buf):
    ts = x_ref.shape[0]
    n = pl.program_id(0)
    slots = ((pool0, xbc0, z0, dt0), (pool1, xbc1, z1, dt1))
    ymix_bufs = (ymix0, ymix1)

    def step(parity, project, mix, finish):
        slot, next_slot = slots[1 - parity], slots[parity]
        ymix_buf = ymix_bufs[1 - parity]
        seq_pos = lax.rem(n - 1, tiles_per_seq)
        starts_seq = seq_pos == 0
        next_starts_seq = seq_pos == tiles_per_seq - 1
        n_chunks = ts // CHUNK
        if finish:
            mixed = jnp.dot(ymix_bufs[parity][...], wout_ref[...], preferred_element_type=F32)
            o_ref[...] = xlag_ref[...] + modlag_ref[2:3, :] * mixed
        if project:
            _project_norm(x_ref, mod_ref, gmix_ref, u_buf)
        if mix:
            _mix_prepare(seq_pos, next_starts_seq, slot, next_slot, dtb_ref, alog_ref, wpool_ref,
                         pscale_ref, p_buf, ac_buf, w_buf, ac_t_buf, ymix_buf)
            for c in range(n_chunks):
                _mix_conv(c, slot[1], convw_ref, convb_ref, act_buf)
        if project:
            for part in range(PROJECT_PARTS):
                _project(part, win_ref, u_buf, next_slot)
        if mix:
            for c in range(n_chunks):
                _mix_chunk(c, starts_seq, slot[2], dskip_ref, gssd_ref, act_buf, ac_buf, w_buf,
                           ac_t_buf, ymix_buf, state_buf, yoff_buf, y_buf)

    @pl.when(n == 0)
    def _():
        pool0[:, 0:POOL_HIST, :] = jnp.zeros((N_POOL, POOL_HIST, LANES), F32)
        xbc0[:, 0:CONV_HIST, :] = jnp.zeros((N_XBC, CONV_HIST, LANES), F32)
        state_buf[...] = jnp.zeros(state_buf.shape, F32)
        step(0, True, False, False)

    @pl.when(n == 1)
    def _():
        step(1, True, True, False)

    steady = (n >= 2) & (n < n_tiles)

    @pl.when(steady & (lax.rem(n, 2) == 0))
    def _():
        step(0, True, True, True)

    @pl.when(steady & (lax.rem(n, 2) == 1))
    def _():
        step(1, True, True, True)

    @pl.when(n == n_tiles)
    def _():
        step(n_tiles % 2, False, True, True)

    @pl.when(n == n_tiles + 1)
    def _():
        step((n_tiles + 1) % 2, False, False, True)


def _mixer(x, mod3, g_mix, w_in_p, conv_w, conv_b, dtb_p, alog_p, dskip_x, g_ssd, w_pool_b,
           pool_scale, w_out_b):
    batch, seq, _ = x.shape
    ts = min(MIX_TILE, seq)
    tiles_per_seq = seq // ts
    n_tiles = batch * tiles_per_seq

    def current(n):
        t = jnp.minimum(n, n_tiles - 1)
        return t // tiles_per_seq, t % tiles_per_seq

    def lagged(n):
        t = jnp.maximum(n - 2, 0)
        return t // tiles_per_seq, t % tiles_per_seq

    in_specs = [
        pl.BlockSpec((None, ts, D_MODEL), lambda n: (*current(n), 0)),
        pl.BlockSpec((None, ts, D_MODEL), lambda n: (*lagged(n), 0)),
        pl.BlockSpec((None, N_MOD, D_MODEL), lambda n: (current(n)[0], 0, 0)),
        pl.BlockSpec((None, N_MOD, D_MODEL), lambda n: (lagged(n)[0], 0, 0)),
        _const_spec((1, D_MODEL)),
        _const_spec((D_MODEL, IN_PAD)),
        _const_spec((CONV_WIDTH, CONV_CH)),
        _const_spec((1, CONV_CH)),
        _const_spec((1, LANES)),
        _const_spec((1, LANES)),
        _const_spec((1, SSD_INNER)),
        _const_spec((1, SSD_INNER)),
        _const_spec((N_POOL, POOL_GROUP, POOL_GROUP)),
        _const_spec((1, POOL_WIDTH)),
        _const_spec((MIX_WIDTH, D_MODEL)),
    ]
    slot = [
        pltpu.VMEM((N_POOL, POOL_HIST + ts, LANES), F32),
        pltpu.VMEM((N_XBC, CONV_HIST + ts, LANES), F32),
        pltpu.VMEM((ts, SSD_INNER), F32),
        pltpu.VMEM((ts, LANES), F32),
    ]
    scratch = [pltpu.VMEM((ts, D_MODEL), BF16)] + slot + slot + [
        pltpu.VMEM((N_POOL, ts, LANES), F32),
        pltpu.VMEM((N_XBC, ts, LANES), F32),
        pltpu.VMEM((ts, LANES), F32),
        pltpu.VMEM((ts, LANES), F32),
        pltpu.VMEM((ts // CHUNK * SSD_HEADS, CHUNK), F32),
        pltpu.VMEM((ts, MIX_WIDTH), BF16),
        pltpu.VMEM((ts, MIX_WIDTH), BF16),
        pltpu.VMEM((SSD_GROUPS, SSD_STATE, GROUP_WIDTH), F32),
        pltpu.VMEM((SSD_GROUPS, CHUNK, GROUP_WIDTH), F32),
        pltpu.VMEM((SSD_GROUPS, CHUNK, GROUP_WIDTH), F32),
    ]
    return pl.pallas_call(
        functools.partial(_mixer_kernel, n_tiles, tiles_per_seq),
        grid=(n_tiles + 2,),
        in_specs=in_specs,
        out_specs=pl.BlockSpec((None, ts, D_MODEL), lambda n: (*lagged(n), 0)),
        out_shape=jax.ShapeDtypeStruct(x.shape, F32),
        scratch_shapes=scratch,
        compiler_params=pltpu.CompilerParams(
            dimension_semantics=("arbitrary",), vmem_limit_bytes=VMEM_LIMIT),
        name="token_mixer",
    )(x, x, mod3, mod3, g_mix, w_in_p, conv_w, conv_b, dtb_p, alog_p, dskip_x, g_ssd, w_pool_b,
      pool_scale, w_out_b)


def _mlp_norm(h_ref, mod_ref, gmlp_ref, u_buf):
    gain = gmlp_ref[...] * (1.0 + mod_ref[4:5, :])
    shift = mod_ref[3:4, :]
    for i in range(h_ref.shape[0] // ROW_BLOCK):
        rows = slice(i * ROW_BLOCK, (i + 1) * ROW_BLOCK)
        hb = h_ref[rows, :]
        ms = jnp.mean(hb * hb, axis=-1, keepdims=True)
        u_buf[rows, :] = (hb * lax.rsqrt(ms + EPS) * gain + shift).astype(BF16)


def _mlp_matmuls(wup_ref, wdown_ref, u_buf, f_buf, acc_buf):
    for j in range(D_FF // FF_CHUNK):
        cols = slice(j * FF_CHUNK, (j + 1) * FF_CHUNK)
        f = jnp.dot(u_buf[...], wup_ref[:, cols], preferred_element_type=F32)
        f_buf[...] = jnp.square(jnp.maximum(f, 0.0)).astype(BF16)
        part = jnp.dot(f_buf[...], wdown_ref[cols, :], preferred_element_type=F32)
        if j == 0:
            acc_buf[...] = part
        else:
            acc_buf[...] += part


def _mlp_finish(h_ref, mod_ref, gfin_ref, acc_buf, o_ref):
    gate = mod_ref[5:6, :]
    for i in range(h_ref.shape[0] // ROW_BLOCK):
        rows = slice(i * ROW_BLOCK, (i + 1) * ROW_BLOCK)
        hb = h_ref[rows, :] + gate * acc_buf[rows, :]
        ms = jnp.mean(hb * hb, axis=-1, keepdims=True)
        o_ref[rows, :] = hb * lax.rsqrt(ms + EPS) * gfin_ref[...]


def _mlp_kernel(n_tiles, h_ref, hlag_ref, mod_ref, modlag_ref, gmlp_ref, wup_ref, wdown_ref,
                gfin_ref, o_ref, u0, u1, acc0, acc1, f_buf):
    n = pl.program_id(0)
    u_bufs = (u0, u1)
    acc_bufs = (acc0, acc1)

    def step(parity, norm, matmuls, finish):
        if finish:
            _mlp_finish(hlag_ref, modlag_ref, gfin_ref, acc_bufs[parity], o_ref)
        if norm:
            _mlp_norm(h_ref, mod_ref, gmlp_ref, u_bufs[parity])
        if matmuls:
            _mlp_matmuls(wup_ref, wdown_ref, u_bufs[1 - parity], f_buf, acc_bufs[1 - parity])

    @pl.when(n == 0)
    def _():
        step(0, True, False, False)

    @pl.when(n == 1)
    def _():
        step(1, True, True, False)

    steady = (n >= 2) & (n < n_tiles)

    @pl.when(steady & (lax.rem(n, 2) == 0))
    def _():
        step(0, True, True, True)

    @pl.when(steady & (lax.rem(n, 2) == 1))
    def _():
        step(1, True, True, True)

    @pl.when(n == n_tiles)
    def _():
        step(n_tiles % 2, False, True, True)

    @pl.when(n == n_tiles + 1)
    def _():
        step((n_tiles + 1) % 2, False, False, True)


def _mlp(h, mod3, g_mlp, w_up_b, w_down_b, g_final):
    batch, seq, _ = h.shape
    tm = min(MLP_TILE, seq)
    tiles_per_seq = seq // tm
    n_tiles = batch * tiles_per_seq
    assert n_tiles >= 2

    def current(n):
        t = jnp.minimum(n, n_tiles - 1)
        return t // tiles_per_seq, t % tiles_per_seq

    def lagged(n):
        t = jnp.maximum(n - 2, 0)
        return t // tiles_per_seq, t % tiles_per_seq

    return pl.pallas_call(
        functools.partial(_mlp_kernel, n_tiles),
        grid=(n_tiles + 2,),
        in_specs=[
            pl.BlockSpec((None, tm, D_MODEL), lambda n: (*current(n), 0)),
            pl.BlockSpec((None, tm, D_MODEL), lambda n: (*lagged(n), 0)),
            pl.BlockSpec((None, N_MOD, D_MODEL), lambda n: (current(n)[0], 0, 0)),
            pl.BlockSpec((None, N_MOD, D_MODEL), lambda n: (lagged(n)[0], 0, 0)),
            _const_spec((1, D_MODEL)),
            _const_spec((D_MODEL, D_FF)),
            _const_spec((D_FF, D_MODEL)),
            _const_spec((1, D_MODEL)),
        ],
        out_specs=pl.BlockSpec((None, tm, D_MODEL), lambda n: (*lagged(n), 0)),
        out_shape=jax.ShapeDtypeStruct(h.shape, F32),
        scratch_shapes=[
            pltpu.VMEM((tm, D_MODEL), BF16),
            pltpu.VMEM((tm, D_MODEL), BF16),
            pltpu.VMEM((tm, D_MODEL), F32),
            pltpu.VMEM((tm, D_MODEL), F32),
            pltpu.VMEM((tm, FF_CHUNK), BF16),
        ],
        compiler_params=pltpu.CompilerParams(
            dimension_semantics=("arbitrary",), vmem_limit_bytes=VMEM_LIMIT),
        name="relu2_mlp",
    )(h, h, mod3, mod3, g_mlp, w_up_b, w_down_b, g_final)


def _pad_lanes(v):
    return jnp.pad(v.reshape(1, -1), ((0, 0), (0, LANES - v.shape[-1])))


def kernel(x, c, w_ada, b_ada, g_mix, w_in, conv_w, conv_b, dt_bias, a_log, d_skip, g_ssd, w_pool,
           pool_scale, w_out, g_mlp, w_up, w_down, g_final):
    batch = x.shape[0]
    assert w_ada.shape[0] == 1, "the MLP call fuses the final norm, so exactly one layer"
    layer = 0
    mod3 = _modulation(c, w_ada[layer], b_ada[layer]).reshape(batch, N_MOD, D_MODEL)
    w_in_p = jnp.pad(w_in[layer], ((0, 0), (0, IN_PAD - w_in.shape[-1]))).astype(BF16)
    h = _mixer(
        x, mod3, g_mix[layer].reshape(1, -1), w_in_p, conv_w[layer],
        conv_b[layer].reshape(1, -1), _pad_lanes(dt_bias[layer]), _pad_lanes(a_log[layer]),
        jnp.repeat(d_skip[layer], SSD_HEAD_DIM).reshape(1, -1), g_ssd[layer].reshape(1, -1),
        w_pool[layer].astype(BF16), pool_scale[layer].reshape(1, -1),
        w_out[layer].astype(BF16))
    return _mlp(h, mod3, g_mlp[layer].reshape(1, -1), w_up[layer].astype(BF16),
                w_down[layer].astype(BF16), g_final.reshape(1, -1))
```

```python
import functools

import jax
import jax.numpy as jnp
from jax import lax
from jax.experimental import pallas as pl
from jax.experimental.pallas import tpu as pltpu

F32 = jnp.float32
BF16 = jnp.bfloat16

D_MODEL = 1024
POOL_WIDTH = 512
POOL_WINDOWS = (2, 4, 8, 16)
POOL_GROUP = 128
SSD_INNER = 1024
SSD_HEADS = 16
SSD_HEAD_DIM = 64
SSD_GROUPS = 2
SSD_STATE = 128
GROUP_WIDTH = SSD_INNER // SSD_GROUPS
CONV_WIDTH = 4
CHUNK = 128
CONV_CH = SSD_INNER + 2 * SSD_GROUPS * SSD_STATE
MIX_WIDTH = POOL_WIDTH + SSD_INNER
OFF_Z = POOL_WIDTH
OFF_XBC = OFF_Z + SSD_INNER
OFF_DT = OFF_XBC + CONV_CH
LANES = 128
IN_PAD = OFF_DT + LANES
D_FF = 4096
N_MOD = 6
EPS = 1e-5

N_POOL = len(POOL_WINDOWS)
N_XBC = CONV_CH // LANES
N_XS = SSD_INNER // LANES
POOL_HIST = 16
CONV_HIST = 8
MIX_TILE = 512
MLP_TILE = 512
FF_CHUNK = 4096
ROW_BLOCK = 64
TOKEN_PHASES = 4
LOG2_E = 1.4426950408889634
VMEM_LIMIT = 56 * 1024 * 1024


def _const_spec(shape):
    zeros = (0,) * len(shape)
    return pl.BlockSpec(shape, lambda *_: zeros, pipeline_mode=pl.Buffered(1))


def _mod_kernel(c_ref, w_ref, b_ref, o_ref):
    c = c_ref[...]
    c_act = (c * jax.nn.sigmoid(c)).astype(BF16)
    o_ref[...] = jnp.dot(c_act, w_ref[...].astype(BF16), preferred_element_type=F32) + b_ref[...]


def _modulation(c, w_ada, b_ada):
    batch = c.shape[0]
    n_out = w_ada.shape[1]
    return pl.pallas_call(
        _mod_kernel,
        grid=(n_out // D_MODEL,),
        in_specs=[
            pl.BlockSpec((batch, D_MODEL), lambda j: (0, 0)),
            pl.BlockSpec((D_MODEL, D_MODEL), lambda j: (0, j)),
            pl.BlockSpec((1, D_MODEL), lambda j: (0, j)),
        ],
        out_specs=pl.BlockSpec((batch, D_MODEL), lambda j: (0, j)),
        out_shape=jax.ShapeDtypeStruct((batch, n_out), F32),
        name="adaln_mod",
    )(c, w_ada, b_ada.reshape(1, n_out))


def _silu(v):
    half = 0.5 * v
    return half + half * jnp.tanh(half)


def _phase_rows(start, n_rows):
    return pl.ds(start, n_rows // TOKEN_PHASES, stride=TOKEN_PHASES)


def _weave(main, *sides):
    for i, _ in enumerate(main):
        for side, every in sides:
            if i % every == 0:
                next(side, None)


def _drain(stream):
    for _ in stream:
        pass


def _chain(*streams):
    for stream in streams:
        yield from stream


def _norm_stream(x_ref, mod_ref, gmix_ref, u_buf):
    ts = x_ref.shape[0]
    gain = gmix_ref[...] * (1.0 + mod_ref[1:2, :])
    shift = mod_ref[0:1, :]
    for i in range(ts // ROW_BLOCK):
        rows = slice(i * ROW_BLOCK, (i + 1) * ROW_BLOCK)
        xb = x_ref[rows, :]
        ms = jnp.mean(xb * xb, axis=-1, keepdims=True)
        u_buf[rows, :] = (xb * lax.rsqrt(ms + EPS) * gain + shift).astype(BF16)
        yield


MXU_COLUMNS = 256


def _project_stream(win_ref, u_buf, slot):
    pool_buf, xbc_buf, z_buf, dt_buf = slot
    ts = u_buf.shape[0]
    slabs = MXU_COLUMNS // LANES

    def dot(lo, hi):
        return jnp.dot(u_buf[...], win_ref[:, lo:hi], preferred_element_type=F32)

    for first in range(0, N_XBC, slabs):
        xbc = dot(OFF_XBC + first * LANES, OFF_XBC + (first + slabs) * LANES)
        for j in range(slabs):
            xbc_buf[first + j, CONV_HIST:CONV_HIST + ts, :] = xbc[:, j * LANES:(j + 1) * LANES]
        yield
    dt_buf[...] = dot(OFF_DT, IN_PAD)
    yield
    for first in range(0, N_POOL, slabs):
        pooled = dot(first * LANES, (first + slabs) * LANES)
        for g in range(slabs):
            pool_buf[first + g, POOL_HIST:POOL_HIST + ts, :] = pooled[:, g * LANES:(g + 1) * LANES]
        yield
    for lo in range(0, SSD_INNER, MXU_COLUMNS):
        z_buf[:, lo:lo + MXU_COLUMNS] = dot(OFF_Z + lo, OFF_Z + lo + MXU_COLUMNS)
        yield


def _out_stream(ymix_buf, wout_ref, xlag_ref, modlag_ref, o_ref):
    for lo in range(0, D_MODEL, MXU_COLUMNS):
        cols = slice(lo, lo + MXU_COLUMNS)
        mixed = jnp.dot(ymix_buf[...], wout_ref[:, cols], preferred_element_type=F32)
        o_ref[:, cols] = xlag_ref[:, cols] + modlag_ref[2:3, cols] * mixed
        yield


def _prepare_stream(seq_pos, next_starts_seq, slot, next_slot, dtb_ref, alog_ref, wpool_ref,
                    pscale_ref, p_buf, ac_buf, w_buf, ac_t_buf, ymix_buf):
    pool_buf, xbc_buf, _, dt_buf = slot
    next_pool_buf, next_xbc_buf, _, _ = next_slot
    ts = dt_buf.shape[0]
    n_chunks = ts // CHUNK

    dt = jax.nn.softplus(dt_buf[...] + dtb_ref[...])
    da = dt * (-LOG2_E * jnp.exp(alog_ref[...]))
    row_id = lax.broadcasted_iota(jnp.int32, (CHUNK, CHUNK), 0)
    col_id = lax.broadcasted_iota(jnp.int32, (CHUNK, CHUNK), 1)
    tril = (row_id >= col_id).astype(BF16)
    tril3 = jnp.concatenate([tril, tril, tril], axis=1)
    for c in range(n_chunks):
        rows = slice(c * CHUNK, (c + 1) * CHUNK)
        hs = slice(c * SSD_HEADS, (c + 1) * SSD_HEADS)
        da_c = da[rows]
        hi = da_c.astype(BF16)
        rest = da_c - hi.astype(F32)
        mid = rest.astype(BF16)
        low = (rest - mid.astype(F32)).astype(BF16)
        ac = jnp.dot(tril3, jnp.concatenate([hi, mid, low], axis=0),
                     preferred_element_type=F32)
        dt_c = dt[rows]
        ac_buf[rows, :] = ac
        w_buf[rows, :] = dt_c * jnp.exp2(ac[CHUNK - 1:CHUNK, :] - ac)
        ac_t_buf[hs, :] = (ac - jnp.log2(dt_c)).T[0:SSD_HEADS, :]
        yield

    phase_rows = ts // TOKEN_PHASES
    token0 = seq_pos * ts + 1 + TOKEN_PHASES * lax.broadcasted_iota(jnp.int32, (phase_rows, LANES), 0)
    tpos = [(token0 + s).astype(F32) for s in range(TOKEN_PHASES)]
    for g, w in enumerate(POOL_WINDOWS):
        lanes = slice(g * POOL_GROUP, (g + 1) * POOL_GROUP)
        view = {m: pool_buf[g, _phase_rows(POOL_HIST + m, ts), :]
                for m in range(1 - w, TOKEN_PHASES)}
        acc = view[0]
        for k in range(1, w):
            acc = acc + view[-k]
        for s in range(TOKEN_PHASES):
            if s > 0:
                acc = acc + view[s] - view[s - w]
            p_buf[g, _phase_rows(s, ts), :] = acc / jnp.minimum(tpos[s], float(w)) - view[s]
        yp = jnp.dot(p_buf[g].astype(BF16), wpool_ref[g], preferred_element_type=F32)
        ymix_buf[:, lanes] = (yp * pscale_ref[:, lanes]).astype(BF16)
        tail = pool_buf[g, ts:ts + POOL_HIST, :]
        next_pool_buf[g, 0:POOL_HIST, :] = jnp.where(next_starts_seq, jnp.zeros_like(tail), tail)
        yield

    for j in range(N_XBC):
        tail = xbc_buf[j, ts:ts + CONV_HIST, :]
        next_xbc_buf[j, 0:CONV_HIST, :] = jnp.where(next_starts_seq, jnp.zeros_like(tail), tail)
    yield


def _conv_stream(c, xbc_buf, convw_ref, convb_ref, act_buf):
    r0 = c * CHUNK
    for j in range(N_XBC):
        lanes = slice(j * LANES, (j + 1) * LANES)
        view = [xbc_buf[j, _phase_rows(CONV_HIST + r0 + m, CHUNK), :]
                for m in range(1 - CONV_WIDTH, TOKEN_PHASES)]
        taps = [convw_ref[k:k + 1, lanes] for k in range(CONV_WIDTH)]
        for s in range(TOKEN_PHASES):
            acc = convb_ref[:, lanes] + view[s] * taps[0]
            for k in range(1, CONV_WIDTH):
                acc = acc + view[s + k] * taps[k]
            act_buf[j, _phase_rows(r0 + s, CHUNK), :] = _silu(acc)
        yield


def _chunk_stream(c, starts_seq, z_buf, dskip_ref, gssd_ref, act_buf, ac_buf, w_buf, ac_t_buf,
                  ymix_buf, state_buf, yoff_buf, y_buf):
    rows = slice(c * CHUNK, (c + 1) * CHUNK)
    row_id = lax.broadcasted_iota(jnp.int32, (CHUNK, CHUNK), 0)
    col_id = lax.broadcasted_iota(jnp.int32, (CHUNK, CHUNK), 1)
    causal = row_id >= col_id
    low_half = col_id < SSD_HEAD_DIM
    slabs_per_group = GROUP_WIDTH // LANES
    heads_per_group = SSD_HEADS // SSD_GROUPS
    ac = ac_buf[rows, :]
    w = w_buf[rows, :]
    ac_t = ac_t_buf[c * SSD_HEADS:(c + 1) * SSD_HEADS, :]

    def head_columns(table, h):
        return [jnp.broadcast_to(table[:, k:k + 1], (CHUNK, LANES)) for k in (h, h + 1)]

    def carried_state(g):
        prev = state_buf[g]
        if c == 0:
            prev = jnp.where(starts_seq, jnp.zeros_like(prev), prev)
        return prev

    for g in range(SSD_GROUPS):
        b_g = act_buf[N_XS + g, rows, :].astype(BF16)
        c_g = act_buf[N_XS + SSD_GROUPS + g, rows, :].astype(BF16)
        scores_bf = lax.dot_general(c_g, b_g, (((1,), (1,)), ((), ())),
                                    preferred_element_type=F32).astype(BF16)
        yoff_buf[g] = jnp.dot(c_g, carried_state(g).astype(BF16), preferred_element_type=F32)
        yield
        ssq, a_last, xdec = None, [], []
        for j in range(slabs_per_group):
            slab = g * slabs_per_group + j
            lanes = slice(slab * LANES, (slab + 1) * LANES)
            group_lanes = slice(j * LANES, (j + 1) * LANES)
            h0 = g * heads_per_group + 2 * j
            xs = act_buf[slab, rows, :]
            ac_cols = head_columns(ac, h0)
            w_cols = head_columns(w, h0)
            acx = jnp.where(low_half, ac_cols[0], ac_cols[1])
            a_last.append(acx[CHUNK - 1:CHUNK, :])
            xdec.append((xs * jnp.where(low_half, w_cols[0], w_cols[1])).astype(BF16))
            m_pair = []
            for k in range(2):
                h = h0 + k
                seg = ac_cols[k] - ac_t[h:h + 1, :]
                decay = jnp.exp2(jnp.where(causal, seg, -jnp.inf))
                m_pair.append(scores_bf * decay.astype(BF16))
            lhs = jnp.concatenate(m_pair, axis=1)
            xp = xs.astype(BF16)
            zero = jnp.zeros_like(xp)
            rhs = jnp.concatenate([jnp.where(low_half, xp, zero),
                                   jnp.where(low_half, zero, xp)], axis=0)
            y = (jnp.dot(lhs, rhs, preferred_element_type=F32)
                 + yoff_buf[g, :, group_lanes] * jnp.exp2(acx) + dskip_ref[:, lanes] * xs)
            y = y * _silu(z_buf[rows, lanes])
            y_buf[g, :, group_lanes] = y
            ssq = y * y if ssq is None else ssq + y * y
            yield
        new_states = lax.dot_general(b_g, jnp.concatenate(xdec, axis=1),
                                     (((0,), (0,)), ((), ())), preferred_element_type=F32)
        chunk_decay = jnp.exp2(jnp.concatenate(a_last, axis=1))
        state_buf[g] = carried_state(g) * chunk_decay + new_states
        inv = lax.rsqrt(jnp.sum(ssq, axis=-1, keepdims=True) * (1.0 / GROUP_WIDTH) + EPS)
        for j in range(slabs_per_group):
            slab = g * slabs_per_group + j
            lanes = slice(slab * LANES, (slab + 1) * LANES)
            ymix_buf[rows, POOL_WIDTH + slab * LANES:POOL_WIDTH + (slab + 1) * LANES] = (
                y_buf[g, :, j * LANES:(j + 1) * LANES] * inv * gssd_ref[:, lanes]).astype(BF16)
        yield


def _mixer_kernel(n_tiles, tiles_per_seq,
                  x_ref, xlag_ref, mod_ref, modlag_ref, gmix_ref, win_ref, convw_ref, convb_ref,
                  dtb_ref, alog_ref, dskip_ref, gssd_ref, wpool_ref, pscale_ref, wout_ref, o_ref,
                  u_buf, pool0, xbc0, z0, dt0, pool1, xbc1, z1, dt1, p_buf, act_buf, ac_buf,
                  w_buf, ac_t_buf, ymix0, ymix1, state_buf, yoff_buf, y_buf):
    ts = x_ref.shape[0]
    n = pl.program_id(0)
    slots = ((pool0, xbc0, z0, dt0), (pool1, xbc1, z1, dt1))
    ymix_bufs = (ymix0, ymix1)

    def step(parity, project, mix, finish):
        slot, next_slot = slots[1 - parity], slots[parity]
        ymix_buf = ymix_bufs[1 - parity]
        seq_pos = lax.rem(n - 1, tiles_per_seq)
        starts_seq = seq_pos == 0
        next_starts_seq = seq_pos == tiles_per_seq - 1
        n_chunks = ts // CHUNK
        none = iter(())

        def conv(c):
            return _conv_stream(c, slot[1], convw_ref, convb_ref, act_buf) if mix else none

        head = [_norm_stream(x_ref, mod_ref, gmix_ref, u_buf)] if project else []
        if mix:
            head.append(_prepare_stream(seq_pos, next_starts_seq, slot, next_slot, dtb_ref,
                                        alog_ref, wpool_ref, pscale_ref, p_buf, ac_buf, w_buf,
                                        ac_t_buf, ymix_buf))
            head.append(conv(0))
        out = (_out_stream(ymix_bufs[parity], wout_ref, xlag_ref, modlag_ref, o_ref)
               if finish else none)
        _weave(_chain(*head), (out, 7))
        _drain(out)

        proj = _project_stream(win_ref, u_buf, next_slot) if project else none
        for c in range(n_chunks if mix else 0):
            chunk = _chunk_stream(c, starts_seq, slot[2], dskip_ref, gssd_ref, act_buf, ac_buf,
                                  w_buf, ac_t_buf, ymix_buf, state_buf, yoff_buf, y_buf)
            next_conv = conv(c + 1) if c + 1 < n_chunks else none
            _weave(chunk, (next_conv, 1), (proj, 4))
            _drain(next_conv)
        _drain(proj)

    @pl.when(n == 0)
    def _():
        pool0[:, 0:POOL_HIST, :] = jnp.zeros((N_POOL, POOL_HIST, LANES), F32)
        xbc0[:, 0:CONV_HIST, :] = jnp.zeros((N_XBC, CONV_HIST, LANES), F32)
        state_buf[...] = jnp.zeros(state_buf.shape, F32)
        step(0, True, False, False)

    @pl.when(n == 1)
    def _():
        step(1, True, True, False)

    steady = (n >= 2) & (n < n_tiles)

    @pl.when(steady & (lax.rem(n, 2) == 0))
    def _():
        step(0, True, True, True)

    @pl.when(steady & (lax.rem(n, 2) == 1))
    def _():
        step(1, True, True, True)

    @pl.when(n == n_tiles)
    def _():
        step(n_tiles % 2, False, True, True)

    @pl.when(n == n_tiles + 1)
    def _():
        step((n_tiles + 1) % 2, False, False, True)


def _mixer(x, mod3, g_mix, w_in_p, conv_w, conv_b, dtb_p, alog_p, dskip_x, g_ssd, w_pool_b,
           pool_scale, w_out_b):
    batch, seq, _ = x.shape
    ts = min(MIX_TILE, seq)
    tiles_per_seq = seq // ts
    n_tiles = batch * tiles_per_seq

    def current(n):
        t = jnp.minimum(n, n_tiles - 1)
        return t // tiles_per_seq, t % tiles_per_seq

    def lagged(n):
        t = jnp.maximum(n - 2, 0)
        return t // tiles_per_seq, t % tiles_per_seq

    in_specs = [
        pl.BlockSpec((None, ts, D_MODEL), lambda n: (*current(n), 0)),
        pl.BlockSpec((None, ts, D_MODEL), lambda n: (*lagged(n), 0)),
        pl.BlockSpec((None, N_MOD, D_MODEL), lambda n: (current(n)[0], 0, 0)),
        pl.BlockSpec((None, N_MOD, D_MODEL), lambda n: (lagged(n)[0], 0, 0)),
        _const_spec((1, D_MODEL)),
        _const_spec((D_MODEL, IN_PAD)),
        _const_spec((CONV_WIDTH, CONV_CH)),
        _const_spec((1, CONV_CH)),
        _const_spec((1, LANES)),
        _const_spec((1, LANES)),
        _const_spec((1, SSD_INNER)),
        _const_spec((1, SSD_INNER)),
        _const_spec((N_POOL, POOL_GROUP, POOL_GROUP)),
        _const_spec((1, POOL_WIDTH)),
        _const_spec((MIX_WIDTH, D_MODEL)),
    ]
    slot = [
        pltpu.VMEM((N_POOL, POOL_HIST + ts, LANES), F32),
        pltpu.VMEM((N_XBC, CONV_HIST + ts, LANES), F32),
        pltpu.VMEM((ts, SSD_INNER), F32),
        pltpu.VMEM((ts, LANES), F32),
    ]
    scratch = [pltpu.VMEM((ts, D_MODEL), BF16)] + slot + slot + [
        pltpu.VMEM((N_POOL, ts, LANES), F32),
        pltpu.VMEM((N_XBC, ts, LANES), F32),
        pltpu.VMEM((ts, LANES), F32),
        pltpu.VMEM((ts, LANES), F32),
        pltpu.VMEM((ts // CHUNK * SSD_HEADS, CHUNK), F32),
        pltpu.VMEM((ts, MIX_WIDTH), BF16),
        pltpu.VMEM((ts, MIX_WIDTH), BF16),
        pltpu.VMEM((SSD_GROUPS, SSD_STATE, GROUP_WIDTH), F32),
        pltpu.VMEM((SSD_GROUPS, CHUNK, GROUP_WIDTH), F32),
        pltpu.VMEM((SSD_GROUPS, CHUNK, GROUP_WIDTH), F32),
    ]
    return pl.pallas_call(
        functools.partial(_mixer_kernel, n_tiles, tiles_per_seq),
        grid=(n_tiles + 2,),
        in_specs=in_specs,
        out_specs=pl.BlockSpec((None, ts, D_MODEL), lambda n: (*lagged(n), 0)),
        out_shape=jax.ShapeDtypeStruct(x.shape, F32),
        scratch_shapes=scratch,
        compiler_params=pltpu.CompilerParams(
            dimension_semantics=("arbitrary",), vmem_limit_bytes=VMEM_LIMIT),
        name="token_mixer",
    )(x, x, mod3, mod3, g_mix, w_in_p, conv_w, conv_b, dtb_p, alog_p, dskip_x, g_ssd, w_pool_b,
      pool_scale, w_out_b)


def _mlp_norm(h_ref, mod_ref, gmlp_ref, u_buf):
    gain = gmlp_ref[...] * (1.0 + mod_ref[4:5, :])
    shift = mod_ref[3:4, :]
    for i in range(h_ref.shape[0] // ROW_BLOCK):
        rows = slice(i * ROW_BLOCK, (i + 1) * ROW_BLOCK)
        hb = h_ref[rows, :]
        ms = jnp.mean(hb * hb, axis=-1, keepdims=True)
        u_buf[rows, :] = (hb * lax.rsqrt(ms + EPS) * gain + shift).astype(BF16)


def _mlp_matmuls(wup_ref, wdown_ref, u_buf, f_buf, acc_buf):
    for j in range(D_FF // FF_CHUNK):
        cols = slice(j * FF_CHUNK, (j + 1) * FF_CHUNK)
        f = jnp.dot(u_buf[...], wup_ref[:, cols], preferred_element_type=F32)
        f_buf[...] = jnp.square(jnp.maximum(f, 0.0)).astype(BF16)
        part = jnp.dot(f_buf[...], wdown_ref[cols, :], preferred_element_type=F32)
        if j == 0:
            acc_buf[...] = part
        else:
            acc_buf[...] += part


def _mlp_finish(h_ref, mod_ref, gfin_ref, acc_buf, o_ref):
    gate = mod_ref[5:6, :]
    for i in range(h_ref.shape[0] // ROW_BLOCK):
        rows = slice(i * ROW_BLOCK, (i + 1) * ROW_BLOCK)
        hb = h_ref[rows, :] + gate * acc_buf[rows, :]
        ms = jnp.mean(hb * hb, axis=-1, keepdims=True)
        o_ref[rows, :] = hb * lax.rsqrt(ms + EPS) * gfin_ref[...]


def _mlp_kernel(n_tiles, h_ref, hlag_ref, mod_ref, modlag_ref, gmlp_ref, wup_ref, wdown_ref,
                gfin_ref, o_ref, u0, u1, acc0, acc1, f_buf):
    n = pl.program_id(0)
    u_bufs = (u0, u1)
    acc_bufs = (acc0, acc1)

    def step(parity, norm, matmuls, finish):
        if finish:
            _mlp_finish(hlag_ref, modlag_ref, gfin_ref, acc_bufs[parity], o_ref)
        if norm:
            _mlp_norm(h_ref, mod_ref, gmlp_ref, u_bufs[parity])
        if matmuls:
            _mlp_matmuls(wup_ref, wdown_ref, u_bufs[1 - parity], f_buf, acc_bufs[1 - parity])

    @pl.when(n == 0)
    def _():
        step(0, True, False, False)

    @pl.when(n == 1)
    def _():
        step(1, True, True, False)

    steady = (n >= 2) & (n < n_tiles)

    @pl.when(steady & (lax.rem(n, 2) == 0))
    def _():
        step(0, True, True, True)

    @pl.when(steady & (lax.rem(n, 2) == 1))
    def _():
        step(1, True, True, True)

    @pl.when(n == n_tiles)
    def _():
        step(n_tiles % 2, False, True, True)

    @pl.when(n == n_tiles + 1)
    def _():
        step((n_tiles + 1) % 2, False, False, True)


def _mlp(h, mod3, g_mlp, w_up_b, w_down_b, g_final):
    batch, seq, _ = h.shape
    tm = min(MLP_TILE, seq)
    tiles_per_seq = seq // tm
    n_tiles = batch * tiles_per_seq
    assert n_tiles >= 2

    def current(n):
        t = jnp.minimum(n, n_tiles - 1)
        return t // tiles_per_seq, t % tiles_per_seq

    def lagged(n):
        t = jnp.maximum(n - 2, 0)
        return t // tiles_per_seq, t % tiles_per_seq

    return pl.pallas_call(
        functools.partial(_mlp_kernel, n_tiles),
        grid=(n_tiles + 2,),
        in_specs=[
            pl.BlockSpec((None, tm, D_MODEL), lambda n: (*current(n), 0)),
            pl.BlockSpec((None, tm, D_MODEL), lambda n: (*lagged(n), 0)),
            pl.BlockSpec((None, N_MOD, D_MODEL), lambda n: (current(n)[0], 0, 0)),
            pl.BlockSpec((None, N_MOD, D_MODEL), lambda n: (lagged(n)[0], 0, 0)),
            _const_spec((1, D_MODEL)),
            _const_spec((D_MODEL, D_FF)),
            _const_spec((D_FF, D_MODEL)),
            _const_spec((1, D_MODEL)),
        ],
        out_specs=pl.BlockSpec((None, tm, D_MODEL), lambda n: (*lagged(n), 0)),
        out_shape=jax.ShapeDtypeStruct(h.shape, F32),
        scratch_shapes=[
            pltpu.VMEM((tm, D_MODEL), BF16),
            pltpu.VMEM((tm, D_MODEL), BF16),
            pltpu.VMEM((tm, D_MODEL), F32),
            pltpu.VMEM((tm, D_MODEL), F32),
            pltpu.VMEM((tm, FF_CHUNK), BF16),
        ],
        compiler_params=pltpu.CompilerParams(
            dimension_semantics=("arbitrary",), vmem_limit_bytes=VMEM_LIMIT),
        name="relu2_mlp",
    )(h, h, mod3, mod3, g_mlp, w_up_b, w_down_b, g_final)


def _pad_lanes(v):
    return jnp.pad(v.reshape(1, -1), ((0, 0), (0, LANES - v.shape[-1])))


def kernel(x, c, w_ada, b_ada, g_mix, w_in, conv_w, conv_b, dt_bias, a_log, d_skip, g_ssd, w_pool,
           pool_scale, w_out, g_mlp, w_up, w_down, g_final):
    batch = x.shape[0]
    assert w_ada.shape[0] == 1, "the MLP call fuses the final norm, so exactly one layer"
    layer = 0
    mod3 = _modulation(c, w_ada[layer], b_ada[layer]).reshape(batch, N_MOD, D_MODEL)
    w_in_p = jnp.pad(w_in[layer], ((0, 0), (0, IN_PAD - w_in.shape[-1]))).astype(BF16)
    h = _mixer(
        x, mod3, g_mix[layer].reshape(1, -1), w_in_p, conv_w[layer],
        conv_b[layer].reshape(1, -1), _pad_lanes(dt_bias[layer]), _pad_lanes(a_log[layer]),
        jnp.repeat(d_skip[layer], SSD_HEAD_DIM).reshape(1, -1), g_ssd[layer].reshape(1, -1),
        w_pool[layer].astype(BF16), pool_scale[layer].reshape(1, -1),
        w_out[layer].astype(BF16))
    return _mlp(h, mod3, g_mlp[layer].reshape(1, -1), w_up[layer].astype(BF16),
                w_down[layer].astype(BF16), g_final.reshape(1, -1))
```
